```python
import math
import jax, jax.numpy as jnp
from jax import lax
import numpy as np

D_MODEL = 2048
BATCH = 16
SEQ = 2048
DEPTH = 4
DEC_BATCH = 2
DEC_SEQ = 16384
PAST_LEN = 128

MIX_WIDTH = D_MODEL
GLA_HEADS = 4
GLA_WIDTH = MIX_WIDTH // 2
GLA_DV = GLA_WIDTH // GLA_HEADS
GLA_DK = GLA_DV // 2
GLA_GATE_RANK = 16
GLA_NORMALIZER = 16.0
GLA_CHUNK = 64
MLA_WIDTH = MIX_WIDTH - GLA_WIDTH
MLA_V = 128
MLA_HEADS = MLA_WIDTH // MLA_V
MLA_NOPE = 128
MLA_ROPE = 64
MLA_Q_RANK = 768
MLA_KV_RANK = 512
ROPE_THETA = 10000.0
Q_BLOCK = 128
N_EXPERTS = 16
EXPERT_FF = D_MODEL
CAPACITY_FACTOR = 2
PLE_DIM = 256
LN_EPS = 1e-5
RMS_EPS = 1e-6
DEEPNORM_ALPHA = (2 * DEPTH) ** 0.25
DEEPNORM_BETA = (8 * DEPTH) ** -0.25
IN_SPLITS = (GLA_HEADS * GLA_DK, GLA_HEADS * GLA_DK, GLA_WIDTH, GLA_WIDTH,
             GLA_GATE_RANK, GLA_GATE_RANK, MLA_Q_RANK, MLA_KV_RANK, MLA_ROPE)
IN_WIDTH = sum(IN_SPLITS)

kernel_name = "hymba_gla_mla_ec_deepnorm_encoder"


def layer_norm(x, g, b):
    xf = x.astype(jnp.float32)
    mu = jnp.mean(xf, axis=-1, keepdims=True)
    var = jnp.mean(jnp.square(xf - mu), axis=-1, keepdims=True)
    y = (xf - mu) * lax.rsqrt(var + LN_EPS) * g.astype(jnp.float32) + b.astype(jnp.float32)
    return y.astype(x.dtype)


def rms_norm(x, g):
    xf = x.astype(jnp.float32)
    y = xf * lax.rsqrt(jnp.mean(jnp.square(xf), axis=-1, keepdims=True) + RMS_EPS) * g.astype(jnp.float32)
    return y.astype(x.dtype)


def rope_tables(S):
    pos = jnp.arange(S, dtype=jnp.float32)
    inv = ROPE_THETA ** (-jnp.arange(0, MLA_ROPE, 2, dtype=jnp.float32) / MLA_ROPE)
    ang = pos[:, None] * inv[None, :]
    return jnp.cos(ang)[:, None, :], jnp.sin(ang)[:, None, :]


def apply_rope(x, cos, sin):
    xf = x.astype(jnp.float32)
    x1, x2 = xf[..., : MLA_ROPE // 2], xf[..., MLA_ROPE // 2:]
    return jnp.concatenate([x1 * cos - x2 * sin, x2 * cos + x1 * sin], axis=-1).astype(x.dtype)


def gla_scan(q, k, v, log_a, strict):
    B, S, H, DK = q.shape
    DV = v.shape[-1]
    n = S // GLA_CHUNK

    def chunks(t):
        return t.reshape(B, n, GLA_CHUNK, H, t.shape[-1]).transpose(1, 0, 3, 2, 4)

    q, k, v, log_a = chunks(q), chunks(k), chunks(v), chunks(log_a)
    b = jnp.cumsum(log_a, axis=3)
    b_last = b[:, :, :, -1:, :]
    q_in = q * jnp.exp(b)
    k_in = k * jnp.exp(-b)
    k_end = k * jnp.exp(b_last - b)
    mask = jnp.tril(jnp.ones((GLA_CHUNK, GLA_CHUNK), dtype=bool), -1 if strict else 0)
    att = jnp.where(mask, jnp.einsum('nbhcd,nbhsd->nbhcs', q_in, k_in), 0.0)
    o_intra = jnp.einsum('nbhcs,nbhse->nbhce', att, v)

    def step(state, inp):
        qc, kc, vc, dl = inp
        o = jnp.einsum('bhcd,bhde->bhce', qc, state)
        state = state * dl[..., None] + jnp.einsum('bhcd,bhce->bhde', kc, vc)
        return state, o

    s0 = jnp.zeros((B, H, DK, DV), jnp.float32)
    _, o_inter = lax.scan(step, s0, (q_in, k_end, v, jnp.exp(b_last[:, :, :, 0, :])))
    o = o_intra + o_inter
    return o.transpose(1, 0, 3, 2, 4).reshape(B, S, H, DV)


def gla_mixer(q_raw, k_raw, v_raw, r_raw, gf_low, gb_low, wg2_f, bg_f, wg2_b, bg_b, norm_g):
    B, S, _ = q_raw.shape
    f32 = jnp.float32
    q = q_raw.reshape(B, S, GLA_HEADS, GLA_DK).astype(f32) * (GLA_DK ** -0.5)
    k = k_raw.reshape(B, S, GLA_HEADS, GLA_DK).astype(f32)
    v = v_raw.reshape(B, S, GLA_HEADS, GLA_DV).astype(f32)
    log_a_f = (jax.nn.log_sigmoid((gf_low @ wg2_f + bg_f).astype(f32)) / GLA_NORMALIZER).reshape(B, S, GLA_HEADS, GLA_DK)
    log_a_b = (jax.nn.log_sigmoid((gb_low @ wg2_b + bg_b).astype(f32)) / GLA_NORMALIZER).reshape(B, S, GLA_HEADS, GLA_DK)
    o_f = gla_scan(q, k, v, log_a_f, False)
    flip = lambda t: t[:, ::-1]
    o_b = flip(gla_scan(flip(q), flip(k), flip(v), flip(log_a_b), True))
    o = rms_norm(o_f + o_b, norm_g)
    return o.reshape(B, S, GLA_WIDTH).astype(r_raw.dtype) * jax.nn.silu(r_raw)


def mla_mixer(cq, ckv, kr, qnorm_g, kvnorm_g, w_uq, w_ukv):
    B, S, _ = cq.shape
    q = (rms_norm(cq, qnorm_g) @ w_uq).reshape(B, S, MLA_HEADS, MLA_NOPE + MLA_ROPE)
    q_nope, q_rope = q[..., :MLA_NOPE], q[..., MLA_NOPE:]
    kv = (rms_norm(ckv, kvnorm_g) @ w_ukv).reshape(B, S, MLA_HEADS, MLA_NOPE + MLA_V)
    k_nope, v = kv[..., :MLA_NOPE], kv[..., MLA_NOPE:]
    cos, sin = rope_tables(S)
    q_rope = apply_rope(q_rope, cos, sin)
    k_rope = apply_rope(kr[:, :, None, :], cos, sin)[:, :, 0, :]
    scale = (MLA_NOPE + MLA_ROPE) ** -0.5
    nq = S // Q_BLOCK

    def blocks(t):
        return t.reshape(B, nq, Q_BLOCK, MLA_HEADS, t.shape[-1]).swapaxes(0, 1)

    def attend(inp):
        qn, qr = inp
        s = (jnp.einsum('bqhd,bkhd->bhqk', qn, k_nope)
             + jnp.einsum('bqhr,bkr->bhqk', qr, k_rope)).astype(jnp.float32) * scale
        p = jax.nn.softmax(s, axis=-1).astype(v.dtype)
        return jnp.einsum('bhqk,bkhd->bqhd', p, v)

    o = lax.map(attend, (blocks(q_nope), blocks(q_rope)))
    return o.swapaxes(0, 1).reshape(B, S, MLA_WIDTH)


def ec_moe(h, w_r, b_r, w_g, w_u, w_d):
    B, S, D = h.shape
    N = B * S
    C = CAPACITY_FACTOR * N // N_EXPERTS
    xt = h.reshape(N, D)
    aff = jax.nn.softmax((xt @ w_r + b_r).astype(jnp.float32), axis=-1)
    gates, idx = lax.top_k(aff.T, C)
    xe = xt[idx]
    hid = jax.nn.silu(jnp.einsum('ecd,edf->ecf', xe, w_g)) * jnp.einsum('ecd,edf->ecf', xe, w_u)
    ye = jnp.einsum('ecf,efd->ecd', hid, w_d) * gates[..., None].astype(h.dtype)
    out = jnp.zeros((N, D), ye.dtype).at[idx.reshape(-1)].add(ye.reshape(-1, D))
    return out.reshape(B, S, D)


def trunk(x, p, ln_in_g, ln_in_b, w_in, gla_wg2_f, gla_bg_f, gla_wg2_b, gla_bg_b, gla_norm_g,
          mla_qnorm_g, mla_kvnorm_g, mla_w_uq, mla_w_ukv, w_o, ln1_g, ln1_b,
          w_router, b_router, w_gate_e, w_up_e, w_down_e, w_ple, w_ple_gate, ln2_g, ln2_b):
    split_points = np.cumsum(IN_SPLITS)[:-1].tolist()
    x = layer_norm(x, ln_in_g, ln_in_b)
    for i in range(DEPTH):
        z = x @ w_in[i]
        qa, ka, va, ra, gf, gb, cq, ckv, kr = jnp.split(z, split_points, axis=-1)
        o_a = gla_mixer(qa, ka, va, ra, gf, gb, gla_wg2_f[i], gla_bg_f[i], gla_wg2_b[i], gla_bg_b[i], gla_norm_g[i])
        o_b = mla_mixer(cq, ckv, kr, mla_qnorm_g[i], mla_kvnorm_g[i], mla_w_uq[i], mla_w_ukv[i])
        mix = jnp.concatenate([o_a, o_b], axis=-1) @ w_o[i]
        h = layer_norm(DEEPNORM_ALPHA * x + mix, ln1_g[i], ln1_b[i])
        moe = ec_moe(h, w_router[i], b_router[i], w_gate_e[i], w_up_e[i], w_down_e[i])
        ple = (p[i] @ w_ple[i]) * jax.nn.sigmoid(h @ w_ple_gate[i])
        x = layer_norm(DEEPNORM_ALPHA * h + moe + ple, ln2_g[i], ln2_b[i])
    return x


def setup_inputs(seed: int = 0) -> dict:
    key = jax.random.key(seed)
    ks = jax.random.split(key, 32)
    f32 = jnp.float32
    L = DEPTH

    def nrm(k, shape, scale):
        return jax.random.normal(k, shape, f32) * scale

    def gain(k, shape):
        return 1.0 + 0.02 * jax.random.normal(k, shape, f32)

    return {
        "x_prompt": nrm(ks[0], (BATCH, SEQ, D_MODEL), 1.0),
        "x_sample": nrm(ks[1], (DEC_BATCH, DEC_SEQ, D_MODEL), 1.0),
        "p_prompt": nrm(ks[2], (DEPTH, BATCH, SEQ, PLE_DIM), 1.0),
        "p_sample": nrm(ks[3], (DEPTH, DEC_BATCH, DEC_SEQ, PLE_DIM), 1.0),
        "ln_in_g": gain(ks[4], (D_MODEL,)),
        "ln_in_b": nrm(ks[5], (D_MODEL,), 0.02),
        "w_in": nrm(ks[6], (L, D_MODEL, IN_WIDTH), D_MODEL ** -0.5),
        "gla_wg2_f": nrm(ks[7], (L, GLA_GATE_RANK, GLA_HEADS * GLA_DK), GLA_GATE_RANK ** -0.5),
        "gla_bg_f": nrm(ks[8], (L, GLA_HEADS * GLA_DK), 0.1),
        "gla_wg2_b": nrm(ks[9], (L, GLA_GATE_RANK, GLA_HEADS * GLA_DK), GLA_GATE_RANK ** -0.5),
        "gla_bg_b": nrm(ks[10], (L, GLA_HEADS * GLA_DK), 0.1),
        "gla_norm_g": gain(ks[11], (L, GLA_DV)),
        "mla_qnorm_g": gain(ks[12], (L, MLA_Q_RANK)),
        "mla_kvnorm_g": gain(ks[13], (L, MLA_KV_RANK)),
        "mla_w_uq": nrm(ks[14], (L, MLA_Q_RANK, MLA_HEADS * (MLA_NOPE + MLA_ROPE)), MLA_Q_RANK ** -0.5),
        "mla_w_ukv": nrm(ks[15], (L, MLA_KV_RANK, MLA_HEADS * (MLA_NOPE + MLA_V)), MLA_KV_RANK ** -0.5),
        "w_o": nrm(ks[16], (L, MIX_WIDTH, D_MODEL), MIX_WIDTH ** -0.5 * DEEPNORM_BETA),
        "ln1_g": gain(ks[17], (L, D_MODEL)),
        "ln1_b": nrm(ks[18], (L, D_MODEL), 0.02),
        "w_router": nrm(ks[19], (L, D_MODEL, N_EXPERTS), D_MODEL ** -0.5),
        "b_router": nrm(ks[20], (L, N_EXPERTS), 0.01),
        "w_gate_e": nrm(ks[21], (L, N_EXPERTS, D_MODEL, EXPERT_FF), D_MODEL ** -0.5),
        "w_up_e": nrm(ks[22], (L, N_EXPERTS, D_MODEL, EXPERT_FF), D_MODEL ** -0.5),
        "w_down_e": nrm(ks[23], (L, N_EXPERTS, EXPERT_FF, D_MODEL), EXPERT_FF ** -0.5 * DEEPNORM_BETA),
        "w_ple": nrm(ks[24], (L, PLE_DIM, D_MODEL), PLE_DIM ** -0.5 * DEEPNORM_BETA),
        "w_ple_gate": nrm(ks[25], (L, D_MODEL, D_MODEL), D_MODEL ** -0.5),
        "ln2_g": gain(ks[26], (L, D_MODEL)),
        "ln2_b": nrm(ks[27], (L, D_MODEL), 0.02),
    }


def reference(x_prompt, x_sample, p_prompt, p_sample, ln_in_g, ln_in_b, w_in,
              gla_wg2_f, gla_bg_f, gla_wg2_b, gla_bg_b, gla_norm_g,
              mla_qnorm_g, mla_kvnorm_g, mla_w_uq, mla_w_ukv, w_o, ln1_g, ln1_b,
              w_router, b_router, w_gate_e, w_up_e, w_down_e, w_ple, w_ple_gate, ln2_g, ln2_b):
    weights = (ln_in_g, ln_in_b, w_in, gla_wg2_f, gla_bg_f, gla_wg2_b, gla_bg_b, gla_norm_g,
               mla_qnorm_g, mla_kvnorm_g, mla_w_uq, mla_w_ukv, w_o, ln1_g, ln1_b,
               w_router, b_router, w_gate_e, w_up_e, w_down_e, w_ple, w_ple_gate, ln2_g, ln2_b)
    y_prompt = trunk(x_prompt, p_prompt, *weights)
    y_sample = trunk(x_sample, p_sample, *weights)
    return (y_prompt, y_sample)
```

```python
import functools
import math

import jax
import jax.numpy as jnp
from jax import lax
from jax.experimental import pallas as pl
from jax.experimental.pallas import tpu as pltpu

F32 = jnp.float32
BF16 = jnp.bfloat16
I32 = jnp.int32

D_MODEL = 2048
GLA_HEADS = 4
GLA_DK = 128
GLA_DV = 256
GLA_WIDTH = GLA_HEADS * GLA_DV
GLA_GATE_RANK = 16
GLA_NORMALIZER = 16.0
GLA_CHUNK = 64
MLA_HEADS = 8
MLA_NOPE = 128
MLA_ROPE = 64
MLA_V = 128
MLA_WIDTH = MLA_HEADS * MLA_V
MLA_Q_RANK = 768
MLA_KV_RANK = 512
ROPE_THETA = 10000.0
N_EXPERTS = 16
CAPACITY_FACTOR = 2
PLE_DIM = 256
LN_EPS = 1e-5
RMS_EPS = 1e-6

LANES = 128
MLA_QK_PAD = 256
VMEM_LIMIT = 56 * 1024 * 1024

COL_Q = 0
COL_K = 512
COL_V = 1024
COL_R = 2048
COL_CQ = 3072
COL_KRA = 3840
COL_KRB = 3968
COL_CKV = 4096
IN_COLS = 4608
GATE_ROW_F = MLA_ROPE
GATE_ROW_B = MLA_ROPE + GLA_GATE_RANK


def _params(sem, vmem=VMEM_LIMIT, **kw):
    return pltpu.CompilerParams(dimension_semantics=sem, vmem_limit_bytes=vmem, **kw)


def _resident(shape, index_map):
    return pl.BlockSpec(shape, index_map, pipeline_mode=pl.Buffered(1))


def _layer_norm(y, g, b):
    mu = jnp.mean(y, axis=-1, keepdims=True)
    yc = y - mu
    var = jnp.mean(yc * yc, axis=-1, keepdims=True)
    return yc * lax.rsqrt(var + LN_EPS) * g + b


def _ln_kernel(x_ref, g_ref, b_ref, o_ref, ob_ref):
    y = _layer_norm(x_ref[...], g_ref[...], b_ref[...])
    o_ref[...] = y
    ob_ref[...] = y.astype(BF16)


def layer_norm_call(x, g, b, tm):
    n, d = x.shape
    return pl.pallas_call(
        _ln_kernel,
        grid=(n // tm,),
        in_specs=[pl.BlockSpec((tm, d), lambda i: (i, 0)),
                  pl.BlockSpec((1, d), lambda i: (0, 0)),
                  pl.BlockSpec((1, d), lambda i: (0, 0))],
        out_specs=[pl.BlockSpec((tm, d), lambda i: (i, 0)),
                   pl.BlockSpec((tm, d), lambda i: (i, 0))],
        out_shape=[jax.ShapeDtypeStruct((n, d), F32), jax.ShapeDtypeStruct((n, d), BF16)],
        compiler_params=_params(("parallel",)),
    )(x, g.reshape(1, d), b.reshape(1, d))


IN_CHUNK = 512


def _inproj_kernel(x_ref, w_ref, o_ref):
    x = x_ref[...]
    for c in range(0, IN_COLS, IN_CHUNK):
        o_ref[:, c:c + IN_CHUNK] = jnp.dot(x, w_ref[:, c:c + IN_CHUNK], preferred_element_type=F32)


def inproj_call(xb, w, tm):
    n, d = xb.shape
    return pl.pallas_call(
        _inproj_kernel,
        grid=(n // tm,),
        in_specs=[pl.BlockSpec((tm, d), lambda i: (i, 0)),
                  _resident((d, IN_COLS), lambda i: (0, 0))],
        out_specs=pl.BlockSpec((tm, IN_COLS), lambda i: (i, 0)),
        out_shape=jax.ShapeDtypeStruct((n, IN_COLS), F32),
        compiler_params=_params(("parallel",)),
    )(xb, w)


MLA_Q_COLS = 3 * LANES


def _rms(x, g):
    return x * lax.rsqrt(jnp.mean(x * x, axis=-1, keepdims=True) + RMS_EPS) * g


def _mla_proj_kernel(cq_ref, kra_ref, krb_ref, ckv_ref, cos_ref, sin_ref, gq_ref, gkv_ref,
                     wuq_ref, wukv_ref, q_ref, k_ref, v_ref):
    cos = cos_ref[...]
    sin = sin_ref[...]
    scale = (MLA_NOPE + MLA_ROPE) ** -0.5
    nq = _rms(cq_ref[...], gq_ref[...]).astype(BF16)
    nkv = _rms(ckv_ref[...], gkv_ref[...]).astype(BF16)
    k_rope = (kra_ref[...] * cos + krb_ref[...] * sin).astype(BF16)
    for h in range(MLA_HEADS):
        qh = jnp.dot(nq, wuq_ref[:, h * MLA_Q_COLS:(h + 1) * MLA_Q_COLS], preferred_element_type=F32)
        q_nope = qh[:, :LANES] * scale
        q_rope = (qh[:, LANES:2 * LANES] * cos + qh[:, 2 * LANES:] * sin) * scale
        q_ref[:, h * MLA_QK_PAD:h * MLA_QK_PAD + LANES] = q_nope.astype(BF16)
        q_ref[:, h * MLA_QK_PAD + LANES:(h + 1) * MLA_QK_PAD] = q_rope.astype(BF16)
        kv = jnp.dot(nkv, wukv_ref[:, h * 2 * LANES:(h + 1) * 2 * LANES], preferred_element_type=F32)
        k_ref[:, h * MLA_QK_PAD:h * MLA_QK_PAD + LANES] = kv[:, :LANES].astype(BF16)
        k_ref[:, h * MLA_QK_PAD + LANES:(h + 1) * MLA_QK_PAD] = k_rope
        v_ref[:, h * MLA_V:(h + 1) * MLA_V] = kv[:, LANES:].astype(BF16)


def mla_proj_call(z, cos_t, sin_t, gq, gkv, wuq, wukv, seq, tm):
    n = z.shape[0]
    nseq = seq // tm
    row = lambda i: (i, 0)
    const = lambda i: (0, 0)
    return pl.pallas_call(
        _mla_proj_kernel,
        grid=(n // tm,),
        in_specs=[pl.BlockSpec((tm, MLA_Q_RANK), lambda i: (i, COL_CQ // MLA_Q_RANK)),
                  pl.BlockSpec((tm, LANES), lambda i: (i, COL_KRA // LANES)),
                  pl.BlockSpec((tm, LANES), lambda i: (i, COL_KRB // LANES)),
                  pl.BlockSpec((tm, MLA_KV_RANK), lambda i: (i, COL_CKV // MLA_KV_RANK)),
                  pl.BlockSpec((tm, LANES), lambda i: (i % nseq, 0)),
                  pl.BlockSpec((tm, LANES), lambda i: (i % nseq, 0)),
                  pl.BlockSpec((1, MLA_Q_RANK), const),
                  pl.BlockSpec((1, MLA_KV_RANK), const),
                  _resident((MLA_Q_RANK, MLA_HEADS * MLA_Q_COLS), const),
                  _resident((MLA_KV_RANK, MLA_HEADS * 2 * LANES), const)],
        out_specs=[pl.BlockSpec((tm, MLA_HEADS * MLA_QK_PAD), row),
                   pl.BlockSpec((tm, MLA_HEADS * MLA_QK_PAD), row),
                   pl.BlockSpec((tm, MLA_WIDTH), row)],
        out_shape=[jax.ShapeDtypeStruct((n, MLA_HEADS * MLA_QK_PAD), BF16),
                   jax.ShapeDtypeStruct((n, MLA_HEADS * MLA_QK_PAD), BF16),
                   jax.ShapeDtypeStruct((n, MLA_WIDTH), BF16)],
        compiler_params=_params(("parallel",)),
    )(z, z, z, z, cos_t, sin_t, gq.reshape(1, -1), gkv.reshape(1, -1), wuq, wukv)


def _flash_kernel(q_ref, k_ref, v_ref, o_ref, m_sc, l_sc, acc_sc):
    j = pl.program_id(3)

    @pl.when(j == 0)
    def _():
        m_sc[...] = jnp.full(m_sc.shape, -jnp.inf, F32)
        l_sc[...] = jnp.zeros(l_sc.shape, F32)
        acc_sc[...] = jnp.zeros(acc_sc.shape, F32)

    s = lax.dot_general(q_ref[...], k_ref[...], (((1,), (1,)), ((), ())), preferred_element_type=F32)
    m_prev = m_sc[...]
    m_new = jnp.maximum(m_prev, jnp.max(s, axis=-1, keepdims=True))
    alpha = jnp.exp(m_prev - m_new)
    p = jnp.exp(s - m_new)
    l_sc[...] = alpha * l_sc[...] + jnp.sum(p, axis=-1, keepdims=True)
    acc_sc[...] = alpha * acc_sc[...] + jnp.dot(p.astype(BF16), v_ref[...], preferred_element_type=F32)
    m_sc[...] = m_new

    @pl.when(j == pl.num_programs(3) - 1)
    def _():
        o_ref[...] = (acc_sc[...] / l_sc[...]).astype(o_ref.dtype)


def flash_call(q, k, v, batch, seq, tq, tk):
    n = q.shape[0]
    nq, nk = seq // tq, seq // tk
    return pl.pallas_call(
        _flash_kernel,
        grid=(batch, MLA_HEADS, nq, nk),
        in_specs=[pl.BlockSpec((tq, MLA_QK_PAD), lambda b, h, i, j: (b * nq + i, h)),
                  pl.BlockSpec((tk, MLA_QK_PAD), lambda b, h, i, j: (b * nk + j, h)),
                  pl.BlockSpec((tk, MLA_V), lambda b, h, i, j: (b * nk + j, h))],
        out_specs=pl.BlockSpec((tq, MLA_V), lambda b, h, i, j: (b * nq + i, h)),
        out_shape=jax.ShapeDtypeStruct((n, MLA_WIDTH), BF16),
        scratch_shapes=[pltpu.VMEM((tq, 1), F32), pltpu.VMEM((tq, 1), F32), pltpu.VMEM((tq, MLA_V), F32)],
        compiler_params=_params(("parallel", "parallel", "parallel", "arbitrary")),
    )(q, k, v)


def _gla_chunk(q, k, v, a, wg, bg, state_t, reverse):
    c = GLA_CHUNK
    hi = lax.Precision.HIGHEST
    pre = jnp.dot(a, wg, preferred_element_type=F32, precision=hi) + bg
    log_a = jax.nn.log_sigmoid(pre) * (1.0 / GLA_NORMALIZER)
    row = lax.broadcasted_iota(I32, (c, c), 0)
    col = lax.broadcasted_iota(I32, (c, c), 1)
    if reverse:
        tri = (col >= row).astype(F32)
        keep = col > row
    else:
        tri = (col <= row).astype(F32)
        keep = col <= row
    b = jnp.dot(tri, log_a, preferred_element_type=F32, precision=hi)
    b_end = b[0:1, :] if reverse else b[c - 1:c, :]
    q_in = (q * (GLA_DK ** -0.5) * jnp.exp(b)).astype(BF16)
    k_in = (k * jnp.exp(-b)).astype(BF16)
    k_end = (k * jnp.exp(b_end - b)).astype(BF16)
    vb = v.astype(BF16)
    att = lax.dot_general(q_in, k_in, (((1,), (1,)), ((), ())), preferred_element_type=F32)
    att = jnp.where(keep, att, 0.0).astype(BF16)
    o = jnp.dot(att, vb, preferred_element_type=F32)
    o = o + lax.dot_general(q_in, state_t.astype(BF16), (((1,), (1,)), ((), ())), preferred_element_type=F32)
    upd = lax.dot_general(vb, k_end, (((0,), (0,)), ((), ())), preferred_element_type=F32)
    state_t = state_t * jnp.exp(b_end) + upd
    return o, state_t


def _gla_fwd_kernel(q_ref, k_ref, v_ref, a_ref, wg_ref, bg_ref, o_ref, st_sc):
    @pl.when(pl.program_id(2) == 0)
    def _():
        st_sc[...] = jnp.zeros(st_sc.shape, F32)

    wg = wg_ref[...]
    bg = bg_ref[...]
    st = st_sc[...]
    for c0 in range(0, q_ref.shape[0], GLA_CHUNK):
        sl = slice(c0, c0 + GLA_CHUNK)
        o, st = _gla_chunk(q_ref[sl, :], k_ref[sl, :], v_ref[sl, :], a_ref[sl, :], wg, bg, st, False)
        o_ref[sl, :] = o
    st_sc[...] = st


def _gla_bwd_kernel(q_ref, k_ref, v_ref, a_ref, wg_ref, bg_ref, of_ref, r_ref, ng_ref, o_ref, st_sc):
    @pl.when(pl.program_id(2) == 0)
    def _():
        st_sc[...] = jnp.zeros(st_sc.shape, F32)

    wg = wg_ref[...]
    bg = bg_ref[...]
    ng = ng_ref[...]
    st = st_sc[...]
    for c0 in reversed(range(0, q_ref.shape[0], GLA_CHUNK)):
        sl = slice(c0, c0 + GLA_CHUNK)
        o, st = _gla_chunk(q_ref[sl, :], k_ref[sl, :], v_ref[sl, :], a_ref[sl, :], wg, bg, st, True)
        o = _rms(o + of_ref[sl, :], ng)
        r = r_ref[sl, :]
        o_ref[sl, :] = (o * (r * jax.nn.sigmoid(r))).astype(o_ref.dtype)
    st_sc[...] = st


def gla_call(z, wg2, bg2, norm_g, batch, seq, t):
    n = z.shape[0]
    nb = seq // t
    qk = lambda off: (lambda b, h, i: (b * nb + i, off // GLA_DK + h))
    vr = lambda off: (lambda b, h, i: (b * nb + i, off // GLA_DV + h))
    common = dict(grid=(batch, GLA_HEADS, nb),
                  scratch_shapes=[pltpu.VMEM((GLA_DV, GLA_DK), F32)],
                  compiler_params=_params(("parallel", "parallel", "arbitrary")))
    o_f = pl.pallas_call(
        _gla_fwd_kernel,
        in_specs=[pl.BlockSpec((t, GLA_DK), qk(COL_Q)),
                  pl.BlockSpec((t, GLA_DK), qk(COL_K)),
                  pl.BlockSpec((t, GLA_DV), vr(COL_V)),
                  pl.BlockSpec((t, LANES), lambda b, h, i: (b * nb + i, COL_KRA // LANES)),
                  pl.BlockSpec((LANES, GLA_DK), lambda b, h, i: (0, h)),
                  pl.BlockSpec((1, GLA_DK), lambda b, h, i: (0, h))],
        out_specs=pl.BlockSpec((t, GLA_DV), lambda b, h, i: (b * nb + i, h)),
        out_shape=jax.ShapeDtypeStruct((n, GLA_WIDTH), F32),
        **common,
    )(z, z, z, z, wg2, bg2)
    rev = lambda f: (lambda b, h, i: f(b, h, nb - 1 - i))
    return pl.pallas_call(
        _gla_bwd_kernel,
        in_specs=[pl.BlockSpec((t, GLA_DK), rev(qk(COL_Q))),
                  pl.BlockSpec((t, GLA_DK), rev(qk(COL_K))),
                  pl.BlockSpec((t, GLA_DV), rev(vr(COL_V))),
                  pl.BlockSpec((t, LANES), rev(lambda b, h, i: (b * nb + i, COL_KRA // LANES))),
                  pl.BlockSpec((LANES, GLA_DK), lambda b, h, i: (0, GLA_HEADS + h)),
                  pl.BlockSpec((1, GLA_DK), lambda b, h, i: (0, GLA_HEADS + h)),
                  pl.BlockSpec((t, GLA_DV), rev(lambda b, h, i: (b * nb + i, h))),
                  pl.BlockSpec((t, GLA_DV), rev(vr(COL_R))),
                  pl.BlockSpec((1, GLA_DV), lambda b, h, i: (0, 0))],
        out_specs=pl.BlockSpec((t, GLA_DV), rev(lambda b, h, i: (b * nb + i, h))),
        out_shape=jax.ShapeDtypeStruct((n, GLA_WIDTH), BF16),
        **common,
    )(z, z, z, z, wg2, bg2, o_f, z, norm_g.reshape(1, GLA_DV))


def _outproj_kernel(alpha, a_ref, b_ref, x_ref, woa_ref, wob_ref, g_ref, bb_ref, wr_ref, br_ref,
                    h_ref, hb_ref, aff_ref):
    mix = jnp.dot(a_ref[...], woa_ref[...], preferred_element_type=F32)
    mix = mix + jnp.dot(b_ref[...], wob_ref[...], preferred_element_type=F32)
    h = _layer_norm(alpha * x_ref[...] + mix, g_ref[...], bb_ref[...])
    h_ref[...] = h
    hb_ref[...] = h.astype(BF16)
    logits = jnp.dot(h, wr_ref[...], preferred_element_type=F32, precision=lax.Precision.HIGHEST) + br_ref[...]
    e = jnp.exp(logits - jnp.max(logits, axis=-1, keepdims=True))
    aff_ref[...] = e / jnp.sum(e, axis=-1, keepdims=True)


def outproj_call(mix_a, mix_b, x, wo_a, wo_b, g, b, wr, br, alpha, tm):
    n, d = x.shape
    row = lambda i: (i, 0)
    const = lambda i: (0, 0)
    return pl.pallas_call(
        functools.partial(_outproj_kernel, alpha),
        grid=(n // tm,),
        in_specs=[pl.BlockSpec((tm, GLA_WIDTH), row),
                  pl.BlockSpec((tm, MLA_WIDTH), row),
                  pl.BlockSpec((tm, d), row),
                  _resident((GLA_WIDTH, d), const),
                  _resident((MLA_WIDTH, d), const),
                  pl.BlockSpec((1, d), const),
                  pl.BlockSpec((1, d), const),
                  _resident((d, N_EXPERTS), const),
                  pl.BlockSpec((1, N_EXPERTS), const)],
        out_specs=[pl.BlockSpec((tm, d), row), pl.BlockSpec((tm, d), row),
                   pl.BlockSpec((tm, N_EXPERTS), row)],
        out_shape=[jax.ShapeDtypeStruct((n, d), F32), jax.ShapeDtypeStruct((n, d), BF16),
                   jax.ShapeDtypeStruct((n, N_EXPERTS), F32)],
        compiler_params=_params(("parallel",)),
    )(mix_a, mix_b, x, wo_a, wo_b, g.reshape(1, d), b.reshape(1, d), wr, br.reshape(1, N_EXPERTS))


def _threshold_kernel(cap, a_ref, thr_ref, need_ref):
    bits = pltpu.bitcast(a_ref[...], I32)

    def body(i, thr):
        cand = thr | jnp.left_shift(jnp.int32(1), 30 - i)
        cnt = jnp.sum((bits >= cand).astype(I32), axis=1, keepdims=True)
        return jnp.where(cnt >= cap, cand, thr)

    thr = lax.fori_loop(0, 31, body, jnp.zeros((N_EXPERTS, 1), I32))
    above = jnp.sum((bits > thr).astype(I32), axis=1, keepdims=True)
    thr_ref[...] = jnp.broadcast_to(thr, thr_ref.shape)
    need_ref[...] = jnp.broadcast_to(cap - above, need_ref.shape)


def threshold_call(aff_t, cap):
    e, n = aff_t.shape
    full = lambda: (0, 0)
    return pl.pallas_call(
        functools.partial(_threshold_kernel, cap),
        in_specs=[pl.BlockSpec((e, n), full)],
        out_specs=[pl.BlockSpec((e, LANES), full), pl.BlockSpec((e, LANES), full)],
        out_shape=[jax.ShapeDtypeStruct((e, LANES), I32), jax.ShapeDtypeStruct((e, LANES), I32)],
        compiler_params=_params(None),
    )(aff_t)


LIST_TID_HI, LIST_TID_LO, LIST_G0, LIST_G1, LIST_G2 = 0, 1, 2, 3, 4
TID_SHIFT = 8
TID_SPLIT = 1 << TID_SHIFT


def _list_kernel(thr_ref, need_ref, a_ref, u_ref, o_ref, carry):
    e = pl.program_id(0)
    j = pl.program_id(1)
    t = a_ref.shape[-1]

    @pl.when(j == 0)
    def _():
        o_ref[...] = jnp.zeros(o_ref.shape, F32)
        carry[0] = 0
        carry[1] = 0

    a = a_ref[...]
    bits = pltpu.bitcast(a, I32)
    thr = thr_ref[e]
    need = need_ref[e]
    gt = bits > thr
    eq = bits == thr
    sub8 = lax.broadcasted_iota(I32, (8, t), 0)
    lhs = jnp.where(sub8 == 0, jnp.where(gt, 1.0, 0.0), jnp.where(sub8 == 1, jnp.where(eq, 1.0, 0.0), 0.0))
    cs = jnp.dot(lhs.astype(BF16), u_ref[...], preferred_element_type=F32).astype(I32)
    c_gt = cs[0:1, :]
    c_eq = cs[1:2, :]
    gt_before = carry[0]
    eq_before = carry[1]
    tie_rank = eq_before + c_eq
    sel = gt | (eq & (tie_rank <= need))
    off = gt_before + jnp.minimum(eq_before, need)
    dest = gt_before + c_gt + jnp.minimum(tie_rank, need) - 1 - off
    dest = jnp.where(sel, dest, -1)
    onehot = (lax.broadcasted_iota(I32, (t, t), 0) == dest).astype(BF16)

    tid = j * t + lax.broadcasted_iota(I32, (1, t), 1)
    g0 = a.astype(BF16).astype(F32)
    g1 = (a - g0).astype(BF16).astype(F32)
    g2 = (a - g0 - g1).astype(BF16).astype(F32)
    subv = lax.broadcasted_iota(I32, (LANES, t), 0)
    vals = jnp.where(subv == LIST_TID_HI, lax.shift_right_logical(tid, TID_SHIFT).astype(F32),
           jnp.where(subv == LIST_TID_LO, (tid & (TID_SPLIT - 1)).astype(F32),
           jnp.where(subv == LIST_G0, g0,
           jnp.where(subv == LIST_G1, g1,
           jnp.where(subv == LIST_G2, g2, 0.0)))))
    rows = lax.dot_general(onehot, vals.astype(BF16), (((1,), (1,)), ((), ())), preferred_element_type=F32)
    win = pl.ds(off, t)
    o_ref[win, :] = o_ref[win, :] + rows
    carry[0] = gt_before + jnp.sum(gt.astype(I32))
    carry[1] = eq_before + jnp.sum(eq.astype(I32))


def list_call(thr, need, aff_t, cap, t):
    e, n = aff_t.shape
    cpad = cap + t
    u = (jnp.arange(t)[:, None] <= jnp.arange(t)[None, :]).astype(BF16)
    return pl.pallas_call(
        _list_kernel,
        grid_spec=pltpu.PrefetchScalarGridSpec(
            num_scalar_prefetch=2,
            grid=(e, n // t),
            in_specs=[pl.BlockSpec((None, 1, t), lambda ei, j, *_: (ei, 0, j)),
                      pl.BlockSpec((t, t), lambda ei, j, *_: (0, 0))],
            out_specs=pl.BlockSpec((None, cpad, LANES), lambda ei, j, *_: (ei, 0, 0)),
            scratch_shapes=[pltpu.SMEM((2,), I32)]),
        out_shape=jax.ShapeDtypeStruct((e, cpad, LANES), F32),
        compiler_params=_params(("arbitrary", "arbitrary")),
    )(thr, need, aff_t.reshape(e, 1, n), u)


def _ple_kernel(alpha, p_ref, hb_ref, h_ref, wp_ref, wg_ref, o_ref):
    ple = jnp.dot(p_ref[...].astype(BF16), wp_ref[...], preferred_element_type=F32)
    gate = jnp.dot(hb_ref[...], wg_ref[...], preferred_element_type=F32)
    o_ref[...] = alpha * h_ref[...] + ple * jax.nn.sigmoid(gate)


def ple_call(p, hb, h, wp, wg, alpha, tm):
    n, d = h.shape
    row = lambda i: (i, 0)
    const = lambda i: (0, 0)
    return pl.pallas_call(
        functools.partial(_ple_kernel, alpha),
        grid=(n // tm,),
        in_specs=[pl.BlockSpec((tm, PLE_DIM), row), pl.BlockSpec((tm, d), row), pl.BlockSpec((tm, d), row),
                  _resident((PLE_DIM, d), const), _resident((d, d), const)],
        out_specs=pl.BlockSpec((tm, d), row),
        out_shape=jax.ShapeDtypeStruct((n, d), F32),
        compiler_params=_params(("parallel",)),
    )(p, hb, h, wp, wg)


def _moe_kernel(idx_ref, lst_ref, wg_ref, wu_ref, wd_ref, h_hbm, acc_in, acc_hbm,
                xbuf, abuf, xsem, asem, ssem):
    del acc_in
    tm = abuf.shape[0]
    nk = pl.num_programs(1)
    step = pl.program_id(0) * nk + pl.program_id(1)
    last = pl.num_programs(0) * nk - 1
    slot = step % 2

    def gather_x(s, sl):
        def body(r, c):
            tok = idx_ref[s * tm + r]
            pltpu.make_async_copy(h_hbm.at[pl.ds(tok, 1), :], xbuf.at[sl, pl.ds(r, 1), :], xsem.at[sl]).start()
            return c
        lax.fori_loop(0, tm, body, 0)

    def rows_done(buf, sem):
        pltpu.make_async_copy(h_hbm.at[pl.ds(0, tm), :], buf, sem).wait()

    @pl.when(step == 0)
    def _():
        gather_x(step, slot)

    @pl.when(step < last)
    def _():
        gather_x(step + 1, 1 - slot)

    rows_done(xbuf.at[slot], xsem.at[slot])
    xb = xbuf[slot].astype(BF16)
    g = jnp.dot(xb, wg_ref[...], preferred_element_type=F32)
    u = jnp.dot(xb, wu_ref[...], preferred_element_type=F32)
    hid = (g * jax.nn.sigmoid(g) * u).astype(BF16)

    @pl.when(step > 0)
    def _():
        rows_done(abuf, ssem)

    def gather_acc(r, c):
        tok = idx_ref[step * tm + r]
        pltpu.make_async_copy(acc_hbm.at[pl.ds(tok, 1), :], abuf.at[pl.ds(r, 1), :], asem).start()
        return c
    lax.fori_loop(0, tm, gather_acc, 0)

    y = jnp.dot(hid, wd_ref[...], preferred_element_type=F32)
    lst = lst_ref[...]
    gate = lst[:, LIST_G0:LIST_G0 + 1] + lst[:, LIST_G1:LIST_G1 + 1] + lst[:, LIST_G2:LIST_G2 + 1]
    y = y * gate
    rows_done(abuf, asem)
    abuf[...] = abuf[...] + y

    def scatter(r, c):
        tok = idx_ref[step * tm + r]
        pltpu.make_async_copy(abuf.at[pl.ds(r, 1), :], acc_hbm.at[pl.ds(tok, 1), :], ssem).start()
        return c
    lax.fori_loop(0, tm, scatter, 0)

    @pl.when(step == last)
    def _():
        rows_done(abuf, ssem)


def moe_call(idx, lists, wg, wu, wd, h, acc, cap, tm):
    n, d = h.shape
    e = wg.shape[0]
    ff = wg.shape[2]
    return pl.pallas_call(
        _moe_kernel,
        grid_spec=pltpu.PrefetchScalarGridSpec(
            num_scalar_prefetch=1,
            grid=(e, cap // tm),
            in_specs=[pl.BlockSpec((None, tm, LANES), lambda ei, k, *_: (ei, k, 0)),
                      _resident((None, d, ff), lambda ei, k, *_: (ei, 0, 0)),
                      _resident((None, d, ff), lambda ei, k, *_: (ei, 0, 0)),
                      _resident((None, ff, d), lambda ei, k, *_: (ei, 0, 0)),
                      pl.BlockSpec(memory_space=pl.ANY),
                      pl.BlockSpec(memory_space=pl.ANY)],
            out_specs=pl.BlockSpec(memory_space=pl.ANY),
            scratch_shapes=[pltpu.VMEM((2, tm, d), F32), pltpu.VMEM((tm, d), F32),
                            pltpu.SemaphoreType.DMA((2,)), pltpu.SemaphoreType.DMA(()),
                            pltpu.SemaphoreType.DMA(())]),
        out_shape=jax.ShapeDtypeStruct((n, d), F32),
        input_output_aliases={6: 0},
        compiler_params=_params(("arbitrary", "arbitrary"), has_side_effects=True),
    )(idx, lists, wg, wu, wd, h, acc)


def _rotate_half_cols(w):
    half = MLA_ROPE // 2
    return jnp.concatenate([-w[..., half:], w[..., :half]], axis=-1)


def _prep_weights(w_in, gla_wg2_f, gla_bg_f, gla_wg2_b, gla_bg_b, mla_w_uq, w_o, w_gate_e, w_up_e,
                  w_down_e, w_ple, w_ple_gate):
    depth = w_in.shape[0]
    gk = GLA_HEADS * GLA_DK
    o = 0
    parts = {}
    for name, width in (("q", gk), ("k", gk), ("v", GLA_WIDTH), ("r", GLA_WIDTH), ("gf", GLA_GATE_RANK),
                        ("gb", GLA_GATE_RANK), ("cq", MLA_Q_RANK), ("ckv", MLA_KV_RANK), ("kr", MLA_ROPE)):
        parts[name] = w_in[:, :, o:o + width]
        o += width
    zeros = lambda w: jnp.zeros((depth, D_MODEL, w), w_in.dtype)
    kra = jnp.concatenate([parts["kr"], parts["gf"], parts["gb"],
                           zeros(LANES - MLA_ROPE - 2 * GLA_GATE_RANK)], axis=-1)
    krb = jnp.concatenate([_rotate_half_cols(parts["kr"]), zeros(LANES - MLA_ROPE)], axis=-1)
    w_in2 = jnp.concatenate([parts["q"], parts["k"], parts["v"], parts["r"], parts["cq"], kra, krb,
                             parts["ckv"]], axis=-1).astype(BF16)

    uq = mla_w_uq.reshape(depth, MLA_Q_RANK, MLA_HEADS, MLA_NOPE + MLA_ROPE)
    rope = uq[..., MLA_NOPE:]
    zpad = jnp.zeros(rope.shape[:-1] + (LANES - MLA_ROPE,), rope.dtype)
    wuq2 = jnp.concatenate([uq[..., :MLA_NOPE], rope, zpad, _rotate_half_cols(rope), zpad], axis=-1)
    wuq2 = wuq2.reshape(depth, MLA_Q_RANK, MLA_HEADS * MLA_Q_COLS).astype(BF16)

    wg2 = jnp.zeros((depth, LANES, 2 * gk), F32)
    wg2 = wg2.at[:, GATE_ROW_F:GATE_ROW_F + GLA_GATE_RANK, :gk].set(gla_wg2_f)
    wg2 = wg2.at[:, GATE_ROW_B:GATE_ROW_B + GLA_GATE_RANK, gk:].set(gla_wg2_b)
    bg2 = jnp.concatenate([gla_bg_f, gla_bg_b], axis=-1).reshape(depth, 1, 2 * gk)
    return dict(w_in=w_in2, wuq=wuq2, wg2=wg2, bg2=bg2,
                wo_a=w_o[:, :GLA_WIDTH].astype(BF16), wo_b=w_o[:, GLA_WIDTH:].astype(BF16),
                wge=w_gate_e.astype(BF16), wue=w_up_e.astype(BF16), wde=w_down_e.astype(BF16),
                wple=w_ple.astype(BF16), wpg=w_ple_gate.astype(BF16))


def _rope_tables(seq):
    pos = jnp.arange(seq, dtype=F32)
    inv = ROPE_THETA ** (-jnp.arange(0, MLA_ROPE, 2, dtype=F32) / MLA_ROPE)
    ang = pos[:, None] * inv[None, :]
    zpad = jnp.zeros((seq, LANES - MLA_ROPE), F32)
    cos_t = jnp.concatenate([jnp.cos(ang), jnp.cos(ang), zpad], axis=-1)
    sin_t = jnp.concatenate([jnp.sin(ang), jnp.sin(ang), zpad], axis=-1)
    return cos_t, sin_t


def _tiles(batch, seq):
    n = batch * seq
    return dict(row=min(512, seq), row_small=min(256, seq), gla=min(256, seq), tq=min(512, seq),
                tk=min(1024, seq), lst=min(256, n), moe=min(256, CAPACITY_FACTOR * n // N_EXPERTS))


def _trunk(x, p, ln_in_g, ln_in_b, pw, gla_norm_g, mla_qnorm_g, mla_kvnorm_g, mla_w_ukv, ln1_g, ln1_b,
           w_router, b_router, ln2_g, ln2_b):
    batch, seq, d = x.shape
    depth = p.shape[0]
    n = batch * seq
    cap = CAPACITY_FACTOR * n // N_EXPERTS
    alpha = (2 * depth) ** 0.25
    tl = _tiles(batch, seq)
    cos_t, sin_t = _rope_tables(seq)
    wukv = mla_w_ukv.astype(BF16)
    xf, xb = layer_norm_call(x.reshape(n, d), ln_in_g, ln_in_b, tl["row"])
    for i in range(depth):
        z = inproj_call(xb, pw["w_in"][i], tl["row"])
        mix_a = gla_call(z, pw["wg2"][i], pw["bg2"][i], gla_norm_g[i], batch, seq, tl["gla"])
        q, k, v = mla_proj_call(z, cos_t, sin_t, mla_qnorm_g[i], mla_kvnorm_g[i], pw["wuq"][i], wukv[i],
                                seq, tl["row"])
        mix_b = flash_call(q, k, v, batch, seq, tl["tq"], tl["tk"])
        h, hb, aff = outproj_call(mix_a, mix_b, xf, pw["wo_a"][i], pw["wo_b"][i], ln1_g[i], ln1_b[i],
                                  w_router[i], b_router[i], alpha, tl["row_small"])
        aff_t = aff.T
        thr, need = threshold_call(aff_t, cap)
        lists = list_call(thr[:, 0], need[:, 0], aff_t, cap, tl["lst"])
        idx = (lists[:, :cap, LIST_TID_HI] * TID_SPLIT + lists[:, :cap, LIST_TID_LO]).astype(I32).reshape(-1)
        acc = ple_call(p[i].reshape(n, PLE_DIM), hb, h, pw["wple"][i], pw["wpg"][i], alpha, tl["row_small"])
        acc = moe_call(idx, lists, pw["wge"][i], pw["wue"][i], pw["wde"][i], h, acc, cap, tl["moe"])
        xf, xb = layer_norm_call(acc, ln2_g[i], ln2_b[i], tl["row"])
    return xf.reshape(batch, seq, d)


def kernel(x_prompt, x_sample, p_prompt, p_sample, ln_in_g, ln_in_b, w_in, gla_wg2_f, gla_bg_f, gla_wg2_b, gla_bg_b, gla_norm_g, mla_qnorm_g, mla_kvnorm_g, mla_w_uq, mla_w_ukv, w_o, ln1_g, ln1_b, w_router, b_router, w_gate_e, w_up_e, w_down_e, w_ple, w_ple_gate, ln2_g, ln2_b):
    pw = _prep_weights(w_in, gla_wg2_f, gla_bg_f, gla_wg2_b, gla_bg_b, mla_w_uq, w_o, w_gate_e, w_up_e,
                       w_down_e, w_ple, w_ple_gate)
    rest = (ln_in_g, ln_in_b, pw, gla_norm_g, mla_qnorm_g, mla_kvnorm_g, mla_w_ukv, ln1_g, ln1_b,
            w_router, b_router, ln2_g, ln2_b)
    return (_trunk(x_prompt, p_prompt, *rest), _trunk(x_sample, p_sample, *rest))
```

```python
import functools
import math

import jax
import jax.numpy as jnp
from jax import lax
from jax.experimental import pallas as pl
from jax.experimental.pallas import tpu as pltpu

F32 = jnp.float32
BF16 = jnp.bfloat16
I32 = jnp.int32

D_MODEL = 2048
GLA_HEADS = 4
GLA_DK = 128
GLA_DV = 256
GLA_WIDTH = GLA_HEADS * GLA_DV
GLA_GATE_RANK = 16
GLA_NORMALIZER = 16.0
GLA_CHUNK = 64
GLA_CHUNK_SHIFT = 6
MLA_HEADS = 8
MLA_NOPE = 128
MLA_ROPE = 64
MLA_V = 128
MLA_WIDTH = MLA_HEADS * MLA_V
MLA_Q_RANK = 768
MLA_KV_RANK = 512
ROPE_THETA = 10000.0
N_EXPERTS = 16
CAPACITY_FACTOR = 2
PLE_DIM = 256
LN_EPS = 1e-5
RMS_EPS = 1e-6

LANES = 128
MLA_QK_PAD = 256
VMEM_LIMIT = 56 * 1024 * 1024

COL_Q = 0
COL_K = 512
COL_V = 1024
COL_R = 2048
COL_CQ = 3072
COL_KRA = 3840
COL_KRB = 3968
COL_CKV = 4096
IN_COLS = 4608
GATE_ROW_F = MLA_ROPE
GATE_ROW_B = MLA_ROPE + GLA_GATE_RANK


def _params(sem, vmem=VMEM_LIMIT, **kw):
    return pltpu.CompilerParams(dimension_semantics=sem, vmem_limit_bytes=vmem, **kw)


def _resident(shape, index_map):
    return pl.BlockSpec(shape, index_map, pipeline_mode=pl.Buffered(1))


def _layer_norm(y, g, b):
    mu = jnp.mean(y, axis=-1, keepdims=True)
    yc = y - mu
    var = jnp.mean(yc * yc, axis=-1, keepdims=True)
    return yc * lax.rsqrt(var + LN_EPS) * g + b


def _ln_kernel(x_ref, g_ref, b_ref, o_ref, ob_ref):
    y = _layer_norm(x_ref[...], g_ref[...], b_ref[...])
    o_ref[...] = y
    ob_ref[...] = y.astype(BF16)


def layer_norm_call(x, g, b, tm):
    n, d = x.shape
    return pl.pallas_call(
        _ln_kernel,
        grid=(n // tm,),
        in_specs=[pl.BlockSpec((tm, d), lambda i: (i, 0)),
                  pl.BlockSpec((1, d), lambda i: (0, 0)),
                  pl.BlockSpec((1, d), lambda i: (0, 0))],
        out_specs=[pl.BlockSpec((tm, d), lambda i: (i, 0)),
                   pl.BlockSpec((tm, d), lambda i: (i, 0))],
        out_shape=[jax.ShapeDtypeStruct((n, d), F32), jax.ShapeDtypeStruct((n, d), BF16)],
        compiler_params=_params(("parallel",)),
    )(x, g.reshape(1, d), b.reshape(1, d))


IN_CHUNK = 512


def _inproj_kernel(x_ref, w_ref, o_ref):
    x = x_ref[...]
    for c in range(0, IN_COLS, IN_CHUNK):
        o_ref[:, c:c + IN_CHUNK] = jnp.dot(x, w_ref[:, c:c + IN_CHUNK], preferred_element_type=F32)


def inproj_call(xb, w, tm):
    n, d = xb.shape
    return pl.pallas_call(
        _inproj_kernel,
        grid=(n // tm,),
        in_specs=[pl.BlockSpec((tm, d), lambda i: (i, 0)),
                  _resident((d, IN_COLS), lambda i: (0, 0))],
        out_specs=pl.BlockSpec((tm, IN_COLS), lambda i: (i, 0)),
        out_shape=jax.ShapeDtypeStruct((n, IN_COLS), F32),
        compiler_params=_params(("parallel",)),
    )(xb, w)


MLA_Q_COLS = 3 * LANES


def _rms(x, g):
    return x * lax.rsqrt(jnp.mean(x * x, axis=-1, keepdims=True) + RMS_EPS) * g


def _mla_proj_kernel(cq_ref, kra_ref, krb_ref, ckv_ref, cos_ref, sin_ref, gq_ref, gkv_ref,
                     wuq_ref, wukv_ref, q_ref, k_ref, v_ref):
    cos = cos_ref[...]
    sin = sin_ref[...]
    scale = (MLA_NOPE + MLA_ROPE) ** -0.5 * math.log2(math.e)
    nq = _rms(cq_ref[...], gq_ref[...]).astype(BF16)
    nkv = _rms(ckv_ref[...], gkv_ref[...]).astype(BF16)
    k_rope = (kra_ref[...] * cos + krb_ref[...] * sin).astype(BF16)
    for h in range(MLA_HEADS):
        qh = jnp.dot(nq, wuq_ref[:, h * MLA_Q_COLS:(h + 1) * MLA_Q_COLS], preferred_element_type=F32)
        q_nope = qh[:, :LANES] * scale
        q_rope = (qh[:, LANES:2 * LANES] * cos + qh[:, 2 * LANES:] * sin) * scale
        q_ref[:, h * MLA_QK_PAD:h * MLA_QK_PAD + LANES] = q_nope.astype(BF16)
        q_ref[:, h * MLA_QK_PAD + LANES:(h + 1) * MLA_QK_PAD] = q_rope.astype(BF16)
        kv = jnp.dot(nkv, wukv_ref[:, h * 2 * LANES:(h + 1) * 2 * LANES], preferred_element_type=F32)
        k_ref[:, h * MLA_QK_PAD:h * MLA_QK_PAD + LANES] = kv[:, :LANES].astype(BF16)
        k_ref[:, h * MLA_QK_PAD + LANES:(h + 1) * MLA_QK_PAD] = k_rope
        v_ref[:, h * MLA_V:(h + 1) * MLA_V] = kv[:, LANES:].astype(BF16)


def mla_proj_call(z, cos_t, sin_t, gq, gkv, wuq, wukv, seq, tm):
    n = z.shape[0]
    nseq = seq // tm
    row = lambda i: (i, 0)
    const = lambda i: (0, 0)
    return pl.pallas_call(
        _mla_proj_kernel,
        grid=(n // tm,),
        in_specs=[pl.BlockSpec((tm, MLA_Q_RANK), lambda i: (i, COL_CQ // MLA_Q_RANK)),
                  pl.BlockSpec((tm, LANES), lambda i: (i, COL_KRA // LANES)),
                  pl.BlockSpec((tm, LANES), lambda i: (i, COL_KRB // LANES)),
                  pl.BlockSpec((tm, MLA_KV_RANK), lambda i: (i, COL_CKV // MLA_KV_RANK)),
                  pl.BlockSpec((tm, LANES), lambda i: (i % nseq, 0)),
                  pl.BlockSpec((tm, LANES), lambda i: (i % nseq, 0)),
                  pl.BlockSpec((1, MLA_Q_RANK), const),
                  pl.BlockSpec((1, MLA_KV_RANK), const),
                  _resident((MLA_Q_RANK, MLA_HEADS * MLA_Q_COLS), const),
                  _resident((MLA_KV_RANK, MLA_HEADS * 2 * LANES), const)],
        out_specs=[pl.BlockSpec((tm, MLA_HEADS * MLA_QK_PAD), row),
                   pl.BlockSpec((tm, MLA_HEADS * MLA_QK_PAD), row),
                   pl.BlockSpec((tm, MLA_WIDTH), row)],
        out_shape=[jax.ShapeDtypeStruct((n, MLA_HEADS * MLA_QK_PAD), BF16),
                   jax.ShapeDtypeStruct((n, MLA_HEADS * MLA_QK_PAD), BF16),
                   jax.ShapeDtypeStruct((n, MLA_WIDTH), BF16)],
        compiler_params=_params(("parallel",)),
    )(z, z, z, z, cos_t, sin_t, gq.reshape(1, -1), gkv.reshape(1, -1), wuq, wukv)


def _flash_kernel(q_ref, k_ref, v_ref, o_ref, m_sc, l_sc, acc_sc):
    j = pl.program_id(3)

    @pl.when(j == 0)
    def _():
        m_sc[...] = jnp.full(m_sc.shape, -jnp.inf, F32)
        l_sc[...] = jnp.zeros(l_sc.shape, F32)
        acc_sc[...] = jnp.zeros(acc_sc.shape, F32)

    s = lax.dot_general(q_ref[...], k_ref[...], (((1,), (1,)), ((), ())), preferred_element_type=F32)
    m_prev = m_sc[...]
    m_new = jnp.maximum(m_prev, jnp.max(s, axis=-1, keepdims=True))
    alpha = jnp.exp2(m_prev - m_new)
    p = jnp.exp2(s - m_new)
    l_sc[...] = alpha * l_sc[...] + jnp.sum(p, axis=-1, keepdims=True)
    acc_sc[...] = alpha * acc_sc[...] + jnp.dot(p.astype(BF16), v_ref[...], preferred_element_type=F32)
    m_sc[...] = m_new

    @pl.when(j == pl.num_programs(3) - 1)
    def _():
        o_ref[...] = (acc_sc[...] / l_sc[...]).astype(o_ref.dtype)


def flash_call(q, k, v, batch, seq, tq, tk):
    n = q.shape[0]
    nq, nk = seq // tq, seq // tk
    return pl.pallas_call(
        _flash_kernel,
        grid=(batch, MLA_HEADS, nq, nk),
        in_specs=[pl.BlockSpec((tq, MLA_QK_PAD), lambda b, h, i, j: (b * nq + i, h)),
                  pl.BlockSpec((tk, MLA_QK_PAD), lambda b, h, i, j: (b * nk + j, h)),
                  pl.BlockSpec((tk, MLA_V), lambda b, h, i, j: (b * nk + j, h))],
        out_specs=pl.BlockSpec((tq, MLA_V), lambda b, h, i, j: (b * nq + i, h)),
        out_shape=jax.ShapeDtypeStruct((n, MLA_WIDTH), BF16),
        scratch_shapes=[pltpu.VMEM((tq, 1), F32), pltpu.VMEM((tq, 1), F32), pltpu.VMEM((tq, MLA_V), F32)],
        compiler_params=_params(("parallel", "parallel", "parallel", "arbitrary")),
    )(q, k, v)


def _gla_block(q, k, v, a, wg, bg, state_t, reverse):
    t = q.shape[0]
    c = GLA_CHUNK
    starts = list(range(0, t, c))
    pre = jnp.dot(a, wg, preferred_element_type=F32, precision=lax.Precision.HIGHEST) + bg
    log_a = jax.nn.log_sigmoid(pre) * (1.0 / GLA_NORMALIZER)
    row = lax.broadcasted_iota(I32, (t, t), 0)
    col = lax.broadcasted_iota(I32, (t, t), 1)
    same = lax.shift_right_logical(row, GLA_CHUNK_SHIFT) == lax.shift_right_logical(col, GLA_CHUNK_SHIFT)
    if reverse:
        tri = same & (col >= row)
        keep = same & (col > row)
    else:
        tri = same & (col <= row)
        keep = same & (col <= row)
    la0 = log_a.astype(BF16)
    r1 = log_a - la0.astype(F32)
    la1 = r1.astype(BF16)
    la2 = (r1 - la1.astype(F32)).astype(BF16)
    b3 = jnp.dot(jnp.where(tri, 1.0, 0.0).astype(BF16), jnp.concatenate([la0, la1, la2], axis=1),
                 preferred_element_type=F32)
    b = b3[:, :GLA_DK] + b3[:, GLA_DK:2 * GLA_DK] + b3[:, 2 * GLA_DK:]
    ends = [b[s0:s0 + 1, :] if reverse else b[s0 + c - 1:s0 + c, :] for s0 in starts]
    b_end = jnp.concatenate([jnp.broadcast_to(e, (c, GLA_DK)) for e in ends], axis=0)
    q_in = (q * (GLA_DK ** -0.5) * jnp.exp(b)).astype(BF16)
    k_in = (k * jnp.exp(-b)).astype(BF16)
    k_end = (k * jnp.exp(b_end - b)).astype(BF16)
    vb = v.astype(BF16)
    att = lax.dot_general(q_in, k_in, (((1,), (1,)), ((), ())), preferred_element_type=F32)
    att = jnp.where(keep, att, 0.0).astype(BF16)
    o_intra = jnp.dot(att, vb, preferred_element_type=F32)
    outs = [None] * len(starts)
    order = reversed(range(len(starts))) if reverse else range(len(starts))
    for ci in order:
        sl = slice(starts[ci], starts[ci] + c)
        outs[ci] = o_intra[sl, :] + lax.dot_general(q_in[sl, :], state_t.astype(BF16), (((1,), (1,)), ((), ())),
                                                    preferred_element_type=F32)
        upd = lax.dot_general(vb[sl, :], k_end[sl, :], (((0,), (0,)), ((), ())), preferred_element_type=F32)
        state_t = state_t * jnp.exp(ends[ci]) + upd
    return jnp.concatenate(outs, axis=0), state_t


def _gla_fwd_kernel(q_ref, k_ref, v_ref, a_ref, wg_ref, bg_ref, o_ref, st_sc):
    @pl.when(pl.program_id(2) == 0)
    def _():
        st_sc[...] = jnp.zeros(st_sc.shape, F32)

    o, st = _gla_block(q_ref[...], k_ref[...], v_ref[...], a_ref[...], wg_ref[...], bg_ref[...], st_sc[...], False)
    o_ref[...] = o
    st_sc[...] = st


def _gla_bwd_kernel(q_ref, k_ref, v_ref, a_ref, wg_ref, bg_ref, of_ref, r_ref, ng_ref, o_ref, st_sc):
    @pl.when(pl.program_id(2) == 0)
    def _():
        st_sc[...] = jnp.zeros(st_sc.shape, F32)

    o, st = _gla_block(q_ref[...], k_ref[...], v_ref[...], a_ref[...], wg_ref[...], bg_ref[...], st_sc[...], True)
    o = _rms(o + of_ref[...], ng_ref[...])
    r = r_ref[...]
    o_ref[...] = (o * (r * jax.nn.sigmoid(r))).astype(o_ref.dtype)
    st_sc[...] = st


def gla_call(z, wg2, bg2, norm_g, batch, seq, t):
    n = z.shape[0]
    nb = seq // t
    qk = lambda off: (lambda b, h, i: (b * nb + i, off // GLA_DK + h))
    vr = lambda off: (lambda b, h, i: (b * nb + i, off // GLA_DV + h))
    common = dict(grid=(batch, GLA_HEADS, nb),
                  scratch_shapes=[pltpu.VMEM((GLA_DV, GLA_DK), F32)],
                  compiler_params=_params(("parallel", "parallel", "arbitrary")))
    o_f = pl.pallas_call(
        _gla_fwd_kernel,
        in_specs=[pl.BlockSpec((t, GLA_DK), qk(COL_Q)),
                  pl.BlockSpec((t, GLA_DK), qk(COL_K)),
                  pl.BlockSpec((t, GLA_DV), vr(COL_V)),
                  pl.BlockSpec((t, LANES), lambda b, h, i: (b * nb + i, COL_KRA // LANES)),
                  pl.BlockSpec((LANES, GLA_DK), lambda b, h, i: (0, h)),
                  pl.BlockSpec((1, GLA_DK), lambda b, h, i: (0, h))],
        out_specs=pl.BlockSpec((t, GLA_DV), lambda b, h, i: (b * nb + i, h)),
        out_shape=jax.ShapeDtypeStruct((n, GLA_WIDTH), F32),
        **common,
    )(z, z, z, z, wg2, bg2)
    rev = lambda f: (lambda b, h, i: f(b, h, nb - 1 - i))
    return pl.pallas_call(
        _gla_bwd_kernel,
        in_specs=[pl.BlockSpec((t, GLA_DK), rev(qk(COL_Q))),
                  pl.BlockSpec((t, GLA_DK), rev(qk(COL_K))),
                  pl.BlockSpec((t, GLA_DV), rev(vr(COL_V))),
                  pl.BlockSpec((t, LANES), rev(lambda b, h, i: (b * nb + i, COL_KRA // LANES))),
                  pl.BlockSpec((LANES, GLA_DK), lambda b, h, i: (0, GLA_HEADS + h)),
                  pl.BlockSpec((1, GLA_DK), lambda b, h, i: (0, GLA_HEADS + h)),
                  pl.BlockSpec((t, GLA_DV), rev(lambda b, h, i: (b * nb + i, h))),
                  pl.BlockSpec((t, GLA_DV), rev(vr(COL_R))),
                  pl.BlockSpec((1, GLA_DV), lambda b, h, i: (0, 0))],
        out_specs=pl.BlockSpec((t, GLA_DV), rev(lambda b, h, i: (b * nb + i, h))),
        out_shape=jax.ShapeDtypeStruct((n, GLA_WIDTH), BF16),
        **common,
    )(z, z, z, z, wg2, bg2, o_f, z, norm_g.reshape(1, GLA_DV))


def _outproj_kernel(alpha, a_ref, b_ref, x_ref, woa_ref, wob_ref, g_ref, bb_ref, wr_ref, br_ref,
                    h_ref, hb_ref, aff_ref):
    mix = jnp.dot(a_ref[...], woa_ref[...], preferred_element_type=F32)
    mix = mix + jnp.dot(b_ref[...], wob_ref[...], preferred_element_type=F32)
    h = _layer_norm(alpha * x_ref[...] + mix, g_ref[...], bb_ref[...])
    h_ref[...] = h
    hb = h.astype(BF16)
    hb_ref[...] = hb
    h_lo = (h - hb.astype(F32)).astype(BF16)
    l1 = jnp.dot(hb, wr_ref[...], preferred_element_type=F32)
    l2 = jnp.dot(h_lo, wr_ref[:, :LANES], preferred_element_type=F32)
    logits = (l1[:, :LANES] + l1[:, LANES:] + l2)[:, :N_EXPERTS] + br_ref[...]
    e = jnp.exp(logits - jnp.max(logits, axis=-1, keepdims=True))
    aff_ref[...] = e / jnp.sum(e, axis=-1, keepdims=True)


def outproj_call(mix_a, mix_b, x, wo_a, wo_b, g, b, wr, br, alpha, tm):
    n, d = x.shape
    row = lambda i: (i, 0)
    const = lambda i: (0, 0)
    return pl.pallas_call(
        functools.partial(_outproj_kernel, alpha),
        grid=(n // tm,),
        in_specs=[pl.BlockSpec((tm, GLA_WIDTH), row),
                  pl.BlockSpec((tm, MLA_WIDTH), row),
                  pl.BlockSpec((tm, d), row),
                  _resident((GLA_WIDTH, d), const),
                  _resident((MLA_WIDTH, d), const),
                  pl.BlockSpec((1, d), const),
                  pl.BlockSpec((1, d), const),
                  _resident((d, 2 * LANES), const),
                  pl.BlockSpec((1, N_EXPERTS), const)],
        out_specs=[pl.BlockSpec((tm, d), row), pl.BlockSpec((tm, d), row),
                   pl.BlockSpec((tm, N_EXPERTS), row)],
        out_shape=[jax.ShapeDtypeStruct((n, d), F32), jax.ShapeDtypeStruct((n, d), BF16),
                   jax.ShapeDtypeStruct((n, N_EXPERTS), F32)],
        compiler_params=_params(("parallel",)),
    )(mix_a, mix_b, x, wo_a, wo_b, g.reshape(1, d), b.reshape(1, d), wr, br.reshape(1, N_EXPERTS))


def _threshold_kernel(cap, a_ref, thr_ref, need_ref):
    bits = pltpu.bitcast(a_ref[...], I32)

    def count(pred):
        part = jnp.sum(pred.astype(I32), axis=1, keepdims=True)
        return jnp.sum(part, axis=2, keepdims=True)

    def body(i, thr):
        cand = thr | jnp.left_shift(jnp.int32(1), 30 - i)
        return jnp.where(count(bits >= cand) >= cap, cand, thr)

    thr = lax.fori_loop(0, 31, body, jnp.zeros((N_EXPERTS, 1, 1), I32))
    thr_ref[...] = jnp.broadcast_to(thr, thr_ref.shape)
    need_ref[...] = jnp.broadcast_to(cap - count(bits > thr), need_ref.shape)


def threshold_call(aff_t, cap):
    e, n = aff_t.shape
    full = lambda: (0, 0, 0)
    return pl.pallas_call(
        functools.partial(_threshold_kernel, cap),
        in_specs=[pl.BlockSpec((e, n // LANES, LANES), full)],
        out_specs=[pl.BlockSpec((e, 1, LANES), full), pl.BlockSpec((e, 1, LANES), full)],
        out_shape=[jax.ShapeDtypeStruct((e, 1, LANES), I32), jax.ShapeDtypeStruct((e, 1, LANES), I32)],
        compiler_params=_params(None),
    )(aff_t.reshape(e, n // LANES, LANES))


LIST_TID_HI, LIST_TID_LO, LIST_G0, LIST_G1, LIST_G2 = 0, 1, 2, 3, 4
TID_SHIFT = 8
TID_SPLIT = 1 << TID_SHIFT


def _list_kernel(thr_ref, need_ref, a_ref, u_ref, o_ref):
    e = pl.program_id(0)
    nt, t = a_ref.shape
    o_ref[...] = jnp.zeros(o_ref.shape, F32)
    thr = thr_ref[e]
    need = need_ref[e]
    u = u_ref[...]
    sub8 = lax.broadcasted_iota(I32, (8, t), 0)
    subv = lax.broadcasted_iota(I32, (LANES, t), 0)
    slot = lax.broadcasted_iota(I32, (t, t), 0)
    lane = lax.broadcasted_iota(I32, (1, t), 1)

    def tile(j, carry):
        gt_before, eq_before = carry
        a = a_ref[pl.ds(j, 1), :]
        bits = pltpu.bitcast(a, I32)
        gt = bits > thr
        eq = bits == thr
        lhs = jnp.where(sub8 == 0, jnp.where(gt, 1.0, 0.0), jnp.where(sub8 == 1, jnp.where(eq, 1.0, 0.0), 0.0))
        cs = jnp.dot(lhs.astype(BF16), u, preferred_element_type=F32).astype(I32)
        c_gt = cs[0:1, :]
        c_eq = cs[1:2, :]
        tie_rank = eq_before + c_eq
        sel = gt | (eq & (tie_rank <= need))
        off = gt_before + jnp.minimum(eq_before, need)
        dest = gt_before + c_gt + jnp.minimum(tie_rank, need) - 1 - off
        dest = jnp.where(sel, dest, -1)
        onehot = (slot == dest).astype(BF16)

        tid = j * t + lane
        g0 = a.astype(BF16).astype(F32)
        g1 = (a - g0).astype(BF16).astype(F32)
        g2 = (a - g0 - g1).astype(BF16).astype(F32)
        vals = jnp.where(subv == LIST_TID_HI, lax.shift_right_logical(tid, TID_SHIFT).astype(F32),
               jnp.where(subv == LIST_TID_LO, (tid & (TID_SPLIT - 1)).astype(F32),
               jnp.where(subv == LIST_G0, g0,
               jnp.where(subv == LIST_G1, g1,
               jnp.where(subv == LIST_G2, g2, 0.0)))))
        rows = lax.dot_general(onehot, vals.astype(BF16), (((1,), (1,)), ((), ())), preferred_element_type=F32)
        win = pl.ds(off, t)
        o_ref[win, :] = o_ref[win, :] + rows
        return gt_before + jnp.sum(gt.astype(I32)), eq_before + jnp.sum(eq.astype(I32))

    lax.fori_loop(0, nt, tile, (jnp.int32(0), jnp.int32(0)))


def list_call(thr, need, aff_t, cap, t):
    e, n = aff_t.shape
    cpad = cap + t
    u = (jnp.arange(t)[:, None] <= jnp.arange(t)[None, :]).astype(BF16)
    return pl.pallas_call(
        _list_kernel,
        grid_spec=pltpu.PrefetchScalarGridSpec(
            num_scalar_prefetch=2,
            grid=(e,),
            in_specs=[pl.BlockSpec((None, n // t, t), lambda ei, *_: (ei, 0, 0)),
                      pl.BlockSpec((t, t), lambda ei, *_: (0, 0))],
            out_specs=pl.BlockSpec((None, cpad, LANES), lambda ei, *_: (ei, 0, 0))),
        out_shape=jax.ShapeDtypeStruct((e, cpad, LANES), F32),
        compiler_params=_params(("arbitrary",)),
    )(thr, need, aff_t.reshape(e, n // t, t), u)


def _ple_kernel(alpha, p_ref, hb_ref, h_ref, wp_ref, wg_ref, o_ref):
    ple = jnp.dot(p_ref[...].astype(BF16), wp_ref[...], preferred_element_type=F32)
    gate = jnp.dot(hb_ref[...], wg_ref[...], preferred_element_type=F32)
    o_ref[...] = alpha * h_ref[...] + ple * jax.nn.sigmoid(gate)


def ple_call(p, hb, h, wp, wg, alpha, tm):
    n, d = h.shape
    row = lambda i: (i, 0)
    const = lambda i: (0, 0)
    return pl.pallas_call(
        functools.partial(_ple_kernel, alpha),
        grid=(n // tm,),
        in_specs=[pl.BlockSpec((tm, PLE_DIM), row), pl.BlockSpec((tm, d), row), pl.BlockSpec((tm, d), row),
                  _resident((PLE_DIM, d), const), _resident((d, d), const)],
        out_specs=pl.BlockSpec((tm, d), row),
        out_shape=jax.ShapeDtypeStruct((n, d), F32),
        compiler_params=_params(("parallel",)),
    )(p, hb, h, wp, wg)


DMA_UNROLL = 8


def _moe_kernel(idx_ref, lst_ref, wg_ref, wu_ref, wd_ref, h_hbm, acc_in, acc_hbm,
                xbuf, abuf, xsem, asem, ssem):
    del acc_in
    tm = abuf.shape[0]
    nk = pl.num_programs(1)
    step = pl.program_id(0) * nk + pl.program_id(1)
    last = pl.num_programs(0) * nk - 1
    slot = step % 2

    def gather_x(s, sl):
        def body(r, c):
            tok = idx_ref[s * tm + r]
            pltpu.make_async_copy(h_hbm.at[pl.ds(tok, 1), :], xbuf.at[sl, pl.ds(r, 1), :], xsem.at[sl]).start()
            return c
        lax.fori_loop(0, tm, body, 0, unroll=DMA_UNROLL)

    def rows_done(buf, sem):
        pltpu.make_async_copy(h_hbm.at[pl.ds(0, tm), :], buf, sem).wait()

    @pl.when(step == 0)
    def _():
        gather_x(step, slot)

    @pl.when(step < last)
    def _():
        gather_x(step + 1, 1 - slot)

    rows_done(xbuf.at[slot], xsem.at[slot])
    xb = xbuf[slot].astype(BF16)
    g = jnp.dot(xb, wg_ref[...], preferred_element_type=F32)
    u = jnp.dot(xb, wu_ref[...], preferred_element_type=F32)
    hid = (g * jax.nn.sigmoid(g) * u).astype(BF16)

    @pl.when(step > 0)
    def _():
        rows_done(abuf, ssem)

    def gather_acc(r, c):
        tok = idx_ref[step * tm + r]
        pltpu.make_async_copy(acc_hbm.at[pl.ds(tok, 1), :], abuf.at[pl.ds(r, 1), :], asem).start()
        return c
    lax.fori_loop(0, tm, gather_acc, 0, unroll=DMA_UNROLL)

    y = jnp.dot(hid, wd_ref[...], preferred_element_type=F32)
    lst = lst_ref[...]
    gate = lst[:, LIST_G0:LIST_G0 + 1] + lst[:, LIST_G1:LIST_G1 + 1] + lst[:, LIST_G2:LIST_G2 + 1]
    y = y * gate
    rows_done(abuf, asem)
    abuf[...] = abuf[...] + y

    def scatter(r, c):
        tok = idx_ref[step * tm + r]
        pltpu.make_async_copy(abuf.at[pl.ds(r, 1), :], acc_hbm.at[pl.ds(tok, 1), :], ssem).start()
        return c
    lax.fori_loop(0, tm, scatter, 0, unroll=DMA_UNROLL)

    @pl.when(step == last)
    def _():
        rows_done(abuf, ssem)


def moe_call(idx, lists, wg, wu, wd, h, acc, cap, tm):
    n, d = h.shape
    e = wg.shape[0]
    ff = wg.shape[2]
    return pl.pallas_call(
        _moe_kernel,
        grid_spec=pltpu.PrefetchScalarGridSpec(
            num_scalar_prefetch=1,
            grid=(e, cap // tm),
            in_specs=[pl.BlockSpec((None, tm, LANES), lambda ei, k, *_: (ei, k, 0)),
                      _resident((None, d, ff), lambda ei, k, *_: (ei, 0, 0)),
                      _resident((None, d, ff), lambda ei, k, *_: (ei, 0, 0)),
                      _resident((None, ff, d), lambda ei, k, *_: (ei, 0, 0)),
                      pl.BlockSpec(memory_space=pl.ANY),
                      pl.BlockSpec(memory_space=pl.ANY)],
            out_specs=pl.BlockSpec(memory_space=pl.ANY),
            scratch_shapes=[pltpu.VMEM((2, tm, d), F32), pltpu.VMEM((tm, d), F32),
                            pltpu.SemaphoreType.DMA((2,)), pltpu.SemaphoreType.DMA(()),
                            pltpu.SemaphoreType.DMA(())]),
        out_shape=jax.ShapeDtypeStruct((n, d), F32),
        input_output_aliases={6: 0},
        compiler_params=_params(("arbitrary", "arbitrary"), has_side_effects=True, disable_bounds_checks=True),
    )(idx, lists, wg, wu, wd, h, acc)


def _rotate_half_cols(w):
    half = MLA_ROPE // 2
    return jnp.concatenate([-w[..., half:], w[..., :half]], axis=-1)


def _prep_weights(w_in, gla_wg2_f, gla_bg_f, gla_wg2_b, gla_bg_b, mla_w_uq, w_o, w_gate_e, w_up_e,
                  w_down_e, w_ple, w_ple_gate):
    depth = w_in.shape[0]
    gk = GLA_HEADS * GLA_DK
    o = 0
    parts = {}
    for name, width in (("q", gk), ("k", gk), ("v", GLA_WIDTH), ("r", GLA_WIDTH), ("gf", GLA_GATE_RANK),
                        ("gb", GLA_GATE_RANK), ("cq", MLA_Q_RANK), ("ckv", MLA_KV_RANK), ("kr", MLA_ROPE)):
        parts[name] = w_in[:, :, o:o + width]
        o += width
    zeros = lambda w: jnp.zeros((depth, D_MODEL, w), w_in.dtype)
    kra = jnp.concatenate([parts["kr"], parts["gf"], parts["gb"],
                           zeros(LANES - MLA_ROPE - 2 * GLA_GATE_RANK)], axis=-1)
    krb = jnp.concatenate([_rotate_half_cols(parts["kr"]), zeros(LANES - MLA_ROPE)], axis=-1)
    w_in2 = jnp.concatenate([parts["q"], parts["k"], parts["v"], parts["r"], parts["cq"], kra, krb,
                             parts["ckv"]], axis=-1).astype(BF16)

    uq = mla_w_uq.reshape(depth, MLA_Q_RANK, MLA_HEADS, MLA_NOPE + MLA_ROPE)
    rope = uq[..., MLA_NOPE:]
    zpad = jnp.zeros(rope.shape[:-1] + (LANES - MLA_ROPE,), rope.dtype)
    wuq2 = jnp.concatenate([uq[..., :MLA_NOPE], rope, zpad, _rotate_half_cols(rope), zpad], axis=-1)
    wuq2 = wuq2.reshape(depth, MLA_Q_RANK, MLA_HEADS * MLA_Q_COLS).astype(BF16)

    wg2 = jnp.zeros((depth, LANES, 2 * gk), F32)
    wg2 = wg2.at[:, GATE_ROW_F:GATE_ROW_F + GLA_GATE_RANK, :gk].set(gla_wg2_f)
    wg2 = wg2.at[:, GATE_ROW_B:GATE_ROW_B + GLA_GATE_RANK, gk:].set(gla_wg2_b)
    bg2 = jnp.concatenate([gla_bg_f, gla_bg_b], axis=-1).reshape(depth, 1, 2 * gk)
    return dict(w_in=w_in2, wuq=wuq2, wg2=wg2, bg2=bg2,
                wo_a=w_o[:, :GLA_WIDTH].astype(BF16), wo_b=w_o[:, GLA_WIDTH:].astype(BF16),
                wge=w_gate_e.astype(BF16), wue=w_up_e.astype(BF16), wde=w_down_e.astype(BF16),
                wple=w_ple.astype(BF16), wpg=w_ple_gate.astype(BF16))


def _rope_tables(seq):
    pos = jnp.arange(seq, dtype=F32)
    inv = ROPE_THETA ** (-jnp.arange(0, MLA_ROPE, 2, dtype=F32) / MLA_ROPE)
    ang = pos[:, None] * inv[None, :]
    zpad = jnp.zeros((seq, LANES - MLA_ROPE), F32)
    cos_t = jnp.concatenate([jnp.cos(ang), jnp.cos(ang), zpad], axis=-1)
    sin_t = jnp.concatenate([jnp.sin(ang), jnp.sin(ang), zpad], axis=-1)
    return cos_t, sin_t


def _tiles(batch, seq):
    n = batch * seq
    return dict(row=min(512, seq), row_small=min(256, seq), gla=min(256, seq), tq=min(512, seq),
                tk=min(8192, seq), lst=min(256, n), moe=min(256, CAPACITY_FACTOR * n // N_EXPERTS))


def _trunk(x, p, ln_in_g, ln_in_b, pw, gla_norm_g, mla_qnorm_g, mla_kvnorm_g, mla_w_ukv, ln1_g, ln1_b,
           w_router, b_router, ln2_g, ln2_b):
    batch, seq, d = x.shape
    depth = p.shape[0]
    n = batch * seq
    cap = CAPACITY_FACTOR * n // N_EXPERTS
    alpha = (2 * depth) ** 0.25
    tl = _tiles(batch, seq)
    cos_t, sin_t = _rope_tables(seq)
    wukv = mla_w_ukv.astype(BF16)
    wr_hi = w_router.astype(BF16)
    wr_lo = (w_router - wr_hi.astype(F32)).astype(BF16)
    wr_pad = jnp.zeros((depth, d, LANES - N_EXPERTS), BF16)
    wr2 = jnp.concatenate([wr_hi, wr_pad, wr_lo, wr_pad], axis=-1)
    xf, xb = layer_norm_call(x.reshape(n, d), ln_in_g, ln_in_b, tl["row"])
    for i in range(depth):
        z = inproj_call(xb, pw["w_in"][i], tl["row"])
        mix_a = gla_call(z, pw["wg2"][i], pw["bg2"][i], gla_norm_g[i], batch, seq, tl["gla"])
        q, k, v = mla_proj_call(z, cos_t, sin_t, mla_qnorm_g[i], mla_kvnorm_g[i], pw["wuq"][i], wukv[i],
                                seq, tl["row"])
        mix_b = flash_call(q, k, v, batch, seq, tl["tq"], tl["tk"])
        h, hb, aff = outproj_call(mix_a, mix_b, xf, pw["wo_a"][i], pw["wo_b"][i], ln1_g[i], ln1_b[i],
                                  wr2[i], b_router[i], alpha, tl["row"])
        aff_t = aff.T
        thr, need = threshold_call(aff_t, cap)
        lists = list_call(thr[:, 0, 0], need[:, 0, 0], aff_t, cap, tl["lst"])
        idx = (lists[:, :cap, LIST_TID_HI] * TID_SPLIT + lists[:, :cap, LIST_TID_LO]).astype(I32).reshape(-1)
        acc = ple_call(p[i].reshape(n, PLE_DIM), hb, h, pw["wple"][i], pw["wpg"][i], alpha, tl["row_small"])
        acc = moe_call(idx, lists, pw["wge"][i], pw["wue"][i], pw["wde"][i], h, acc, cap, tl["moe"])
        xf, xb = layer_norm_call(acc, ln2_g[i], ln2_b[i], tl["row"])
    return xf.reshape(batch, seq, d)


def kernel(x_prompt, x_sample, p_prompt, p_sample, ln_in_g, ln_in_b, w_in, gla_wg2_f, gla_bg_f, gla_wg2_b, gla_bg_b, gla_norm_g, mla_qnorm_g, mla_kvnorm_g, mla_w_uq, mla_w_ukv, w_o, ln1_g, ln1_b, w_router, b_router, w_gate_e, w_up_e, w_down_e, w_ple, w_ple_gate, ln2_g, ln2_b):
    pw = _prep_weights(w_in, gla_wg2_f, gla_bg_f, gla_wg2_b, gla_bg_b, mla_w_uq, w_o, w_gate_e, w_up_e,
                       w_down_e, w_ple, w_ple_gate)
    rest = (ln_in_g, ln_in_b, pw, gla_norm_g, mla_qnorm_g, mla_kvnorm_g, mla_w_ukv, ln1_g, ln1_b,
            w_router, b_router, ln2_g, ln2_b)
    return (_trunk(x_prompt, p_prompt, *rest), _trunk(x_sample, p_sample, *rest))
```

```python
import functools
import math

import jax
import jax.numpy as jnp
from jax import lax
from jax.experimental import pallas as pl
from jax.experimental.pallas import tpu as pltpu

F32 = jnp.float32
BF16 = jnp.bfloat16
I32 = jnp.int32

D_MODEL = 2048
GLA_HEADS = 4
GLA_DK = 128
GLA_DV = 256
GLA_WIDTH = GLA_HEADS * GLA_DV
GLA_GATE_RANK = 16
GLA_NORMALIZER = 16.0
GLA_CHUNK = 64
GLA_CHUNK_SHIFT = 6
MLA_HEADS = 8
MLA_NOPE = 128
MLA_ROPE = 64
MLA_V = 128
MLA_WIDTH = MLA_HEADS * MLA_V
MLA_Q_RANK = 768
MLA_KV_RANK = 512
ROPE_THETA = 10000.0
N_EXPERTS = 16
CAPACITY_FACTOR = 2
PLE_DIM = 256
LN_EPS = 1e-5
RMS_EPS = 1e-6

LANES = 128
MLA_QK_PAD = 256
VMEM_LIMIT = 56 * 1024 * 1024

COL_Q = 0
COL_K = 512
COL_V = 1024
COL_R = 2048
COL_CQ = 3072
COL_KRA = 3840
COL_KRB = 3968
COL_CKV = 4096
IN_COLS = 4608
GATE_ROW_F = MLA_ROPE
GATE_ROW_B = MLA_ROPE + GLA_GATE_RANK


def _params(sem, vmem=VMEM_LIMIT, **kw):
    return pltpu.CompilerParams(dimension_semantics=sem, vmem_limit_bytes=vmem, **kw)


def _resident(shape, index_map):
    return pl.BlockSpec(shape, index_map, pipeline_mode=pl.Buffered(1))


def _layer_norm(y, g, b):
    mu = jnp.mean(y, axis=-1, keepdims=True)
    yc = y - mu
    var = jnp.mean(yc * yc, axis=-1, keepdims=True)
    return yc * lax.rsqrt(var + LN_EPS) * g + b


def _ln_kernel(x_ref, g_ref, b_ref, o_ref, ob_ref):
    y = _layer_norm(x_ref[...], g_ref[...], b_ref[...])
    o_ref[...] = y
    ob_ref[...] = y.astype(BF16)


def layer_norm_call(x, g, b, tm):
    n, d = x.shape
    return pl.pallas_call(
        _ln_kernel,
        grid=(n // tm,),
        in_specs=[pl.BlockSpec((tm, d), lambda i: (i, 0)),
                  pl.BlockSpec((1, d), lambda i: (0, 0)),
                  pl.BlockSpec((1, d), lambda i: (0, 0))],
        out_specs=[pl.BlockSpec((tm, d), lambda i: (i, 0)),
                   pl.BlockSpec((tm, d), lambda i: (i, 0))],
        out_shape=[jax.ShapeDtypeStruct((n, d), F32), jax.ShapeDtypeStruct((n, d), BF16)],
        compiler_params=_params(("parallel",)),
    )(x, g.reshape(1, d), b.reshape(1, d))


IN_CHUNK = 512


def _inproj_kernel(x_ref, w_ref, o_ref):
    x = x_ref[...]
    for c in range(0, IN_COLS, IN_CHUNK):
        o_ref[:, c:c + IN_CHUNK] = jnp.dot(x, w_ref[:, c:c + IN_CHUNK], preferred_element_type=F32)


def inproj_call(xb, w, tm):
    n, d = xb.shape
    return pl.pallas_call(
        _inproj_kernel,
        grid=(n // tm,),
        in_specs=[pl.BlockSpec((tm, d), lambda i: (i, 0)),
                  _resident((d, IN_COLS), lambda i: (0, 0))],
        out_specs=pl.BlockSpec((tm, IN_COLS), lambda i: (i, 0)),
        out_shape=jax.ShapeDtypeStruct((n, IN_COLS), F32),
        compiler_params=_params(("parallel",)),
    )(xb, w)


MLA_Q_COLS = 3 * LANES


def _rms(x, g):
    return x * lax.rsqrt(jnp.mean(x * x, axis=-1, keepdims=True) + RMS_EPS) * g


def _mla_proj_kernel(cq_ref, kra_ref, krb_ref, ckv_ref, cos_ref, sin_ref, gq_ref, gkv_ref,
                     wuq_ref, wukv_ref, q_ref, k_ref, v_ref):
    cos = cos_ref[...]
    sin = sin_ref[...]
    scale = (MLA_NOPE + MLA_ROPE) ** -0.5 * math.log2(math.e)
    nq = _rms(cq_ref[...], gq_ref[...]).astype(BF16)
    nkv = _rms(ckv_ref[...], gkv_ref[...]).astype(BF16)
    k_rope = (kra_ref[...] * cos + krb_ref[...] * sin).astype(BF16)
    for h in range(MLA_HEADS):
        qh = jnp.dot(nq, wuq_ref[:, h * MLA_Q_COLS:(h + 1) * MLA_Q_COLS], preferred_element_type=F32)
        q_nope = qh[:, :LANES] * scale
        q_rope = (qh[:, LANES:2 * LANES] * cos + qh[:, 2 * LANES:] * sin) * scale
        q_ref[:, h * MLA_QK_PAD:h * MLA_QK_PAD + LANES] = q_nope.astype(BF16)
        q_ref[:, h * MLA_QK_PAD + LANES:(h + 1) * MLA_QK_PAD] = q_rope.astype(BF16)
        kv = jnp.dot(nkv, wukv_ref[:, h * 2 * LANES:(h + 1) * 2 * LANES], preferred_element_type=F32)
        k_ref[:, h * MLA_QK_PAD:h * MLA_QK_PAD + LANES] = kv[:, :LANES].astype(BF16)
        k_ref[:, h * MLA_QK_PAD + LANES:(h + 1) * MLA_QK_PAD] = k_rope
        v_ref[:, h * MLA_V:(h + 1) * MLA_V] = kv[:, LANES:].astype(BF16)


def mla_proj_call(z, cos_t, sin_t, gq, gkv, wuq, wukv, seq, tm):
    n = z.shape[0]
    nseq = seq // tm
    row = lambda i: (i, 0)
    const = lambda i: (0, 0)
    return pl.pallas_call(
        _mla_proj_kernel,
        grid=(n // tm,),
        in_specs=[pl.BlockSpec((tm, MLA_Q_RANK), lambda i: (i, COL_CQ // MLA_Q_RANK)),
                  pl.BlockSpec((tm, LANES), lambda i: (i, COL_KRA // LANES)),
                  pl.BlockSpec((tm, LANES), lambda i: (i, COL_KRB // LANES)),
                  pl.BlockSpec((tm, MLA_KV_RANK), lambda i: (i, COL_CKV // MLA_KV_RANK)),
                  pl.BlockSpec((tm, LANES), lambda i: (i % nseq, 0)),
                  pl.BlockSpec((tm, LANES), lambda i: (i % nseq, 0)),
                  pl.BlockSpec((1, MLA_Q_RANK), const),
                  pl.BlockSpec((1, MLA_KV_RANK), const),
                  _resident((MLA_Q_RANK, MLA_HEADS * MLA_Q_COLS), const),
                  _resident((MLA_KV_RANK, MLA_HEADS * 2 * LANES), const)],
        out_specs=[pl.BlockSpec((tm, MLA_HEADS * MLA_QK_PAD), row),
                   pl.BlockSpec((tm, MLA_HEADS * MLA_QK_PAD), row),
                   pl.BlockSpec((tm, MLA_WIDTH), row)],
        out_shape=[jax.ShapeDtypeStruct((n, MLA_HEADS * MLA_QK_PAD), BF16),
                   jax.ShapeDtypeStruct((n, MLA_HEADS * MLA_QK_PAD), BF16),
                   jax.ShapeDtypeStruct((n, MLA_WIDTH), BF16)],
        compiler_params=_params(("parallel",)),
    )(z, z, z, z, cos_t, sin_t, gq.reshape(1, -1), gkv.reshape(1, -1), wuq, wukv)


def _flash_kernel(q_ref, k_ref, v_ref, o_ref, m_sc, l_sc, acc_sc):
    j = pl.program_id(3)

    @pl.when(j == 0)
    def _():
        m_sc[...] = jnp.full(m_sc.shape, -jnp.inf, F32)
        l_sc[...] = jnp.zeros(l_sc.shape, F32)
        acc_sc[...] = jnp.zeros(acc_sc.shape, F32)

    s = lax.dot_general(q_ref[...], k_ref[...], (((1,), (1,)), ((), ())), preferred_element_type=F32)
    m_prev = m_sc[...]
    m_new = jnp.maximum(m_prev, jnp.max(s, axis=-1, keepdims=True))
    alpha = jnp.exp2(m_prev - m_new)
    p = jnp.exp2(s - m_new)
    l_sc[...] = alpha * l_sc[...] + jnp.sum(p, axis=-1, keepdims=True)
    acc_sc[...] = alpha * acc_sc[...] + jnp.dot(p.astype(BF16), v_ref[...], preferred_element_type=F32)
    m_sc[...] = m_new

    @pl.when(j == pl.num_programs(3) - 1)
    def _():
        o_ref[...] = (acc_sc[...] / l_sc[...]).astype(o_ref.dtype)


def flash_call(q, k, v, batch, seq, tq, tk):
    n = q.shape[0]
    nq, nk = seq // tq, seq // tk
    return pl.pallas_call(
        _flash_kernel,
        grid=(batch, MLA_HEADS, nq, nk),
        in_specs=[pl.BlockSpec((tq, MLA_QK_PAD), lambda b, h, i, j: (b * nq + i, h)),
                  pl.BlockSpec((tk, MLA_QK_PAD), lambda b, h, i, j: (b * nk + j, h)),
                  pl.BlockSpec((tk, MLA_V), lambda b, h, i, j: (b * nk + j, h))],
        out_specs=pl.BlockSpec((tq, MLA_V), lambda b, h, i, j: (b * nq + i, h)),
        out_shape=jax.ShapeDtypeStruct((n, MLA_WIDTH), BF16),
        scratch_shapes=[pltpu.VMEM((tq, 1), F32), pltpu.VMEM((tq, 1), F32), pltpu.VMEM((tq, MLA_V), F32)],
        compiler_params=_params(("parallel", "parallel", "parallel", "arbitrary")),
    )(q, k, v)


def _gla_block(q, k, v, a, wg, bg, states, reverse):
    t = q.shape[0]
    c = GLA_CHUNK
    gk = GLA_HEADS * GLA_DK
    heads = range(GLA_HEADS)
    ks = [slice(h * GLA_DK, (h + 1) * GLA_DK) for h in heads]
    vs = [slice(h * GLA_DV, (h + 1) * GLA_DV) for h in heads]
    nt = (((1,), (1,)), ((), ()))
    starts = list(range(0, t, c))
    pre = jnp.dot(a, wg, preferred_element_type=F32, precision=lax.Precision.HIGHEST) + bg
    log_a = jax.nn.log_sigmoid(pre) * (1.0 / GLA_NORMALIZER)
    row = lax.broadcasted_iota(I32, (t, t), 0)
    col = lax.broadcasted_iota(I32, (t, t), 1)
    same = lax.shift_right_logical(row, GLA_CHUNK_SHIFT) == lax.shift_right_logical(col, GLA_CHUNK_SHIFT)
    if reverse:
        tri = same & (col >= row)
        keep = same & (col > row)
    else:
        tri = same & (col <= row)
        keep = same & (col <= row)
    la0 = log_a.astype(BF16)
    r1 = log_a - la0.astype(F32)
    la1 = r1.astype(BF16)
    la2 = (r1 - la1.astype(F32)).astype(BF16)
    b3 = jnp.dot(jnp.where(tri, 1.0, 0.0).astype(BF16), jnp.concatenate([la0, la1, la2], axis=1),
                 preferred_element_type=F32)
    b = b3[:, :gk] + b3[:, gk:2 * gk] + b3[:, 2 * gk:]
    ends = [b[s0:s0 + 1, :] if reverse else b[s0 + c - 1:s0 + c, :] for s0 in starts]
    b_end = jnp.concatenate([jnp.broadcast_to(e, (c, gk)) for e in ends], axis=0)
    q_in = (q * (GLA_DK ** -0.5) * jnp.exp(b)).astype(BF16)
    k_in = (k * jnp.exp(-b)).astype(BF16)
    k_end = (k * jnp.exp(b_end - b)).astype(BF16)
    decay = [jnp.exp(e) for e in ends]
    vb = v.astype(BF16)
    att = [lax.dot_general(q_in[:, ks[h]], k_in[:, ks[h]], nt, preferred_element_type=F32) for h in heads]
    att = [jnp.where(keep, x, 0.0).astype(BF16) for x in att]
    o_intra = [jnp.dot(att[h], vb[:, vs[h]], preferred_element_type=F32) for h in heads]
    upd = [[lax.dot_general(vb[s0:s0 + c, vs[h]], k_end[s0:s0 + c, ks[h]], (((0,), (0,)), ((), ())),
                            preferred_element_type=F32) for h in heads] for s0 in starts]
    outs = [[None] * len(starts) for _ in heads]
    states = list(states)
    order = reversed(range(len(starts))) if reverse else range(len(starts))
    for ci in order:
        sl = slice(starts[ci], starts[ci] + c)
        for h in heads:
            outs[h][ci] = o_intra[h][sl, :] + lax.dot_general(q_in[sl, ks[h]], states[h].astype(BF16), nt,
                                                              preferred_element_type=F32)
            states[h] = states[h] * decay[ci][:, ks[h]] + upd[ci][h]
    return [jnp.concatenate(o, axis=0) for o in outs], states


def _gla_heads(q_ref, k_ref, v_ref, a_ref, wg_ref, bg_ref, st_sc, reverse):
    @pl.when(pl.program_id(1) == 0)
    def _():
        st_sc[...] = jnp.zeros(st_sc.shape, F32)

    outs, states = _gla_block(q_ref[...], k_ref[...], v_ref[...], a_ref[...], wg_ref[...], bg_ref[...],
                              [st_sc[h] for h in range(GLA_HEADS)], reverse)
    for h, st in enumerate(states):
        st_sc[h] = st
    return outs


def _gla_fwd_kernel(q_ref, k_ref, v_ref, a_ref, wg_ref, bg_ref, o_ref, st_sc):
    outs = _gla_heads(q_ref, k_ref, v_ref, a_ref, wg_ref, bg_ref, st_sc, False)
    for h, o in enumerate(outs):
        o_ref[:, h * GLA_DV:(h + 1) * GLA_DV] = o


def _gla_bwd_kernel(q_ref, k_ref, v_ref, a_ref, wg_ref, bg_ref, of_ref, r_ref, ng_ref, o_ref, st_sc):
    outs = _gla_heads(q_ref, k_ref, v_ref, a_ref, wg_ref, bg_ref, st_sc, True)
    ng = ng_ref[...]
    for h, o in enumerate(outs):
        vv = slice(h * GLA_DV, (h + 1) * GLA_DV)
        o = _rms(o + of_ref[:, vv], ng)
        r = r_ref[:, vv]
        o_ref[:, vv] = (o * (r * jax.nn.sigmoid(r))).astype(o_ref.dtype)


def gla_call(z, wg2, bg2, norm_g, batch, seq, t):
    n = z.shape[0]
    nb = seq // t
    gk = GLA_HEADS * GLA_DK
    fwd = lambda col: (lambda b, i: (b * nb + i, col))
    rev = lambda col: (lambda b, i: (b * nb + nb - 1 - i, col))
    common = dict(grid=(batch, nb),
                  scratch_shapes=[pltpu.VMEM((GLA_HEADS, GLA_DV, GLA_DK), F32)],
                  compiler_params=_params(("parallel", "arbitrary")))

    def z_specs(at, direction):
        return [pl.BlockSpec((t, gk), at(COL_Q // gk)),
                pl.BlockSpec((t, gk), at(COL_K // gk)),
                pl.BlockSpec((t, GLA_WIDTH), at(COL_V // GLA_WIDTH)),
                pl.BlockSpec((t, LANES), at(COL_KRA // LANES)),
                pl.BlockSpec((LANES, gk), lambda b, i: (0, direction)),
                pl.BlockSpec((1, gk), lambda b, i: (0, direction))]

    o_f = pl.pallas_call(
        _gla_fwd_kernel,
        in_specs=z_specs(fwd, 0),
        out_specs=pl.BlockSpec((t, GLA_WIDTH), fwd(0)),
        out_shape=jax.ShapeDtypeStruct((n, GLA_WIDTH), F32),
        **common,
    )(z, z, z, z, wg2, bg2)
    return pl.pallas_call(
        _gla_bwd_kernel,
        in_specs=z_specs(rev, 1) + [pl.BlockSpec((t, GLA_WIDTH), rev(0)),
                                    pl.BlockSpec((t, GLA_WIDTH), rev(COL_R // GLA_WIDTH)),
                                    pl.BlockSpec((1, GLA_DV), lambda b, i: (0, 0))],
        out_specs=pl.BlockSpec((t, GLA_WIDTH), rev(0)),
        out_shape=jax.ShapeDtypeStruct((n, GLA_WIDTH), BF16),
        **common,
    )(z, z, z, z, wg2, bg2, o_f, z, norm_g.reshape(1, GLA_DV))


def _outproj_kernel(alpha, a_ref, b_ref, x_ref, woa_ref, wob_ref, g_ref, bb_ref, wr_ref, br_ref,
                    h_ref, hb_ref, aff_ref):
    mix = jnp.dot(a_ref[...], woa_ref[...], preferred_element_type=F32)
    mix = mix + jnp.dot(b_ref[...], wob_ref[...], preferred_element_type=F32)
    h = _layer_norm(alpha * x_ref[...] + mix, g_ref[...], bb_ref[...])
    h_ref[...] = h
    hb = h.astype(BF16)
    hb_ref[...] = hb
    h_lo = (h - hb.astype(F32)).astype(BF16)
    l1 = jnp.dot(hb, wr_ref[...], preferred_element_type=F32)
    l2 = jnp.dot(h_lo, wr_ref[:, :LANES], preferred_element_type=F32)
    logits = (l1[:, :LANES] + l1[:, LANES:] + l2)[:, :N_EXPERTS] + br_ref[...]
    e = jnp.exp(logits - jnp.max(logits, axis=-1, keepdims=True))
    aff_ref[...] = e / jnp.sum(e, axis=-1, keepdims=True)


def outproj_call(mix_a, mix_b, x, wo_a, wo_b, g, b, wr, br, alpha, tm):
    n, d = x.shape
    row = lambda i: (i, 0)
    const = lambda i: (0, 0)
    return pl.pallas_call(
        functools.partial(_outproj_kernel, alpha),
        grid=(n // tm,),
        in_specs=[pl.BlockSpec((tm, GLA_WIDTH), row),
                  pl.BlockSpec((tm, MLA_WIDTH), row),
                  pl.BlockSpec((tm, d), row),
                  _resident((GLA_WIDTH, d), const),
                  _resident((MLA_WIDTH, d), const),
                  pl.BlockSpec((1, d), const),
                  pl.BlockSpec((1, d), const),
                  _resident((d, 2 * LANES), const),
                  pl.BlockSpec((1, N_EXPERTS), const)],
        out_specs=[pl.BlockSpec((tm, d), row), pl.BlockSpec((tm, d), row),
                   pl.BlockSpec((tm, N_EXPERTS), row)],
        out_shape=[jax.ShapeDtypeStruct((n, d), F32), jax.ShapeDtypeStruct((n, d), BF16),
                   jax.ShapeDtypeStruct((n, N_EXPERTS), F32)],
        compiler_params=_params(("parallel",)),
    )(mix_a, mix_b, x, wo_a, wo_b, g.reshape(1, d), b.reshape(1, d), wr, br.reshape(1, N_EXPERTS))


def _threshold_kernel(cap, a_ref, thr_ref, need_ref):
    bits = pltpu.bitcast(a_ref[...], I32)

    def count(pred):
        part = jnp.sum(pred.astype(I32), axis=1, keepdims=True)
        return jnp.sum(part, axis=2, keepdims=True)

    def body(i, thr):
        cand = thr | jnp.left_shift(jnp.int32(1), 30 - i)
        return jnp.where(count(bits >= cand) >= cap, cand, thr)

    thr = lax.fori_loop(0, 31, body, jnp.zeros((N_EXPERTS, 1, 1), I32))
    thr_ref[...] = jnp.broadcast_to(thr, thr_ref.shape)
    need_ref[...] = jnp.broadcast_to(cap - count(bits > thr), need_ref.shape)


def threshold_call(aff_t, cap):
    e, n = aff_t.shape
    full = lambda: (0, 0, 0)
    return pl.pallas_call(
        functools.partial(_threshold_kernel, cap),
        in_specs=[pl.BlockSpec((e, n // LANES, LANES), full)],
        out_specs=[pl.BlockSpec((e, 1, LANES), full), pl.BlockSpec((e, 1, LANES), full)],
        out_shape=[jax.ShapeDtypeStruct((e, 1, LANES), I32), jax.ShapeDtypeStruct((e, 1, LANES), I32)],
        compiler_params=_params(None),
    )(aff_t.reshape(e, n // LANES, LANES))


LIST_TID_HI, LIST_TID_LO, LIST_G0, LIST_G1, LIST_G2 = 0, 1, 2, 3, 4
TID_SHIFT = 8
TID_SPLIT = 1 << TID_SHIFT


LIST_GROUP = 4


def _list_kernel(thr_ref, need_ref, a_ref, u_ref, o_ref):
    e0 = pl.program_id(0) * LIST_GROUP
    _, nt, t = a_ref.shape
    o_ref[...] = jnp.zeros(o_ref.shape, F32)
    u = u_ref[...]
    sub8 = lax.broadcasted_iota(I32, (8, t), 0)
    subv = lax.broadcasted_iota(I32, (LANES, t), 0)
    slot = lax.broadcasted_iota(I32, (t, t), 0)
    lane = lax.broadcasted_iota(I32, (1, t), 1)

    def one_expert(g, j, gt_before, eq_before):
        thr = thr_ref[e0 + g]
        need = need_ref[e0 + g]
        a = a_ref[g, pl.ds(j, 1), :]
        bits = pltpu.bitcast(a, I32)
        gt = bits > thr
        eq = bits == thr
        lhs = jnp.where(sub8 == 0, jnp.where(gt, 1.0, 0.0), jnp.where(sub8 == 1, jnp.where(eq, 1.0, 0.0), 0.0))
        cs = jnp.dot(lhs.astype(BF16), u, preferred_element_type=F32).astype(I32)
        c_gt = cs[0:1, :]
        c_eq = cs[1:2, :]
        tie_rank = eq_before + c_eq
        sel = gt | (eq & (tie_rank <= need))
        off = gt_before + jnp.minimum(eq_before, need)
        dest = gt_before + c_gt + jnp.minimum(tie_rank, need) - 1 - off
        dest = jnp.where(sel, dest, -1)
        onehot = (slot == dest).astype(BF16)

        tid = j * t + lane
        g0 = a.astype(BF16).astype(F32)
        g1 = (a - g0).astype(BF16).astype(F32)
        g2 = (a - g0 - g1).astype(BF16).astype(F32)
        vals = jnp.where(subv == LIST_TID_HI, lax.shift_right_logical(tid, TID_SHIFT).astype(F32),
               jnp.where(subv == LIST_TID_LO, (tid & (TID_SPLIT - 1)).astype(F32),
               jnp.where(subv == LIST_G0, g0,
               jnp.where(subv == LIST_G1, g1,
               jnp.where(subv == LIST_G2, g2, 0.0)))))
        rows = lax.dot_general(onehot, vals.astype(BF16), (((1,), (1,)), ((), ())), preferred_element_type=F32)
        win = pl.ds(off, t)
        o_ref[g, win, :] = o_ref[g, win, :] + rows
        return gt_before + jnp.sum(gt.astype(I32)), eq_before + jnp.sum(eq.astype(I32))

    def tile(j, carry):
        out = []
        for g in range(LIST_GROUP):
            out.extend(one_expert(g, j, carry[2 * g], carry[2 * g + 1]))
        return tuple(out)

    lax.fori_loop(0, nt, tile, (jnp.int32(0),) * (2 * LIST_GROUP))


def list_call(thr, need, aff_t, cap, t):
    e, n = aff_t.shape
    cpad = cap + t
    u = (jnp.arange(t)[:, None] <= jnp.arange(t)[None, :]).astype(BF16)
    return pl.pallas_call(
        _list_kernel,
        grid_spec=pltpu.PrefetchScalarGridSpec(
            num_scalar_prefetch=2,
            grid=(e // LIST_GROUP,),
            in_specs=[pl.BlockSpec((LIST_GROUP, n // t, t), lambda ei, *_: (ei, 0, 0)),
                      pl.BlockSpec((t, t), lambda ei, *_: (0, 0))],
            out_specs=pl.BlockSpec((LIST_GROUP, cpad, LANES), lambda ei, *_: (ei, 0, 0))),
        out_shape=jax.ShapeDtypeStruct((e, cpad, LANES), F32),
        compiler_params=_params(("arbitrary",)),
    )(thr, need, aff_t.reshape(e, n // t, t), u)


def _ple_kernel(alpha, p_ref, hb_ref, h_ref, wp_ref, wg_ref, o_ref):
    ple = jnp.dot(p_ref[...].astype(BF16), wp_ref[...], preferred_element_type=F32)
    gate = jnp.dot(hb_ref[...], wg_ref[...], preferred_element_type=F32)
    o_ref[...] = alpha * h_ref[...] + ple * jax.nn.sigmoid(gate)


def ple_call(p, hb, h, wp, wg, alpha, tm):
    n, d = h.shape
    row = lambda i: (i, 0)
    const = lambda i: (0, 0)
    return pl.pallas_call(
        functools.partial(_ple_kernel, alpha),
        grid=(n // tm,),
        in_specs=[pl.BlockSpec((tm, PLE_DIM), row), pl.BlockSpec((tm, d), row), pl.BlockSpec((tm, d), row),
                  _resident((PLE_DIM, d), const), _resident((d, d), const)],
        out_specs=pl.BlockSpec((tm, d), row),
        out_shape=jax.ShapeDtypeStruct((n, d), F32),
        compiler_params=_params(("parallel",)),
    )(p, hb, h, wp, wg)


def _moe_kernel(idx_ref, lst_ref, wg_ref, wu_ref, wd_ref, h_hbm, acc_in, acc_hbm,
                xbuf, abuf, xsem, asem, ssem):
    del acc_in
    tm = abuf.shape[0]
    nk = pl.num_programs(1)
    step = pl.program_id(0) * nk + pl.program_id(1)
    last = pl.num_programs(0) * nk - 1
    slot = step % 2

    def gather_x(s, sl):
        for r in range(tm):
            tok = idx_ref[s * tm + r]
            pltpu.make_async_copy(h_hbm.at[pl.ds(tok, 1), :], xbuf.at[sl, pl.ds(r, 1), :], xsem.at[sl]).start()

    def rows_done(buf, sem):
        pltpu.make_async_copy(h_hbm.at[pl.ds(0, tm), :], buf, sem).wait()

    @pl.when(step == 0)
    def _():
        gather_x(step, slot)

    rows_done(xbuf.at[slot], xsem.at[slot])
    gather_x(jnp.minimum(step + 1, last), 1 - slot)
    xb = xbuf[slot].astype(BF16)
    g = jnp.dot(xb, wg_ref[...], preferred_element_type=F32)
    u = jnp.dot(xb, wu_ref[...], preferred_element_type=F32)
    hid = (g * jax.nn.sigmoid(g) * u).astype(BF16)

    @pl.when(step > 0)
    def _():
        rows_done(abuf, ssem)

    for r in range(tm):
        tok = idx_ref[step * tm + r]
        pltpu.make_async_copy(acc_hbm.at[pl.ds(tok, 1), :], abuf.at[pl.ds(r, 1), :], asem).start()

    y = jnp.dot(hid, wd_ref[...], preferred_element_type=F32)
    lst = lst_ref[...]
    gate = lst[:, LIST_G0:LIST_G0 + 1] + lst[:, LIST_G1:LIST_G1 + 1] + lst[:, LIST_G2:LIST_G2 + 1]
    y = y * gate
    rows_done(abuf, asem)
    abuf[...] = abuf[...] + y

    for r in range(tm):
        tok = idx_ref[step * tm + r]
        pltpu.make_async_copy(abuf.at[pl.ds(r, 1), :], acc_hbm.at[pl.ds(tok, 1), :], ssem).start()

    @pl.when(step == last)
    def _():
        rows_done(abuf, ssem)
        rows_done(xbuf.at[1 - slot], xsem.at[1 - slot])


def moe_call(idx, lists, wg, wu, wd, h, acc, cap, tm):
    n, d = h.shape
    e = wg.shape[0]
    ff = wg.shape[2]
    return pl.pallas_call(
        _moe_kernel,
        grid_spec=pltpu.PrefetchScalarGridSpec(
            num_scalar_prefetch=1,
            grid=(e, cap // tm),
            in_specs=[pl.BlockSpec((None, tm, LANES), lambda ei, k, *_: (ei, k, 0)),
                      _resident((None, d, ff), lambda ei, k, *_: (ei, 0, 0)),
                      _resident((None, d, ff), lambda ei, k, *_: (ei, 0, 0)),
                      _resident((None, ff, d), lambda ei, k, *_: (ei, 0, 0)),
                      pl.BlockSpec(memory_space=pl.ANY),
                      pl.BlockSpec(memory_space=pl.ANY)],
            out_specs=pl.BlockSpec(memory_space=pl.ANY),
            scratch_shapes=[pltpu.VMEM((2, tm, d), F32), pltpu.VMEM((tm, d), F32),
                            pltpu.SemaphoreType.DMA((2,)), pltpu.SemaphoreType.DMA(()),
                            pltpu.SemaphoreType.DMA(())]),
        out_shape=jax.ShapeDtypeStruct((n, d), F32),
        input_output_aliases={6: 0},
        compiler_params=_params(("arbitrary", "arbitrary"), has_side_effects=True, disable_bounds_checks=True),
    )(idx, lists, wg, wu, wd, h, acc)


def _rotate_half_cols(w):
    half = MLA_ROPE // 2
    return jnp.concatenate([-w[..., half:], w[..., :half]], axis=-1)


def _prep_weights(w_in, gla_wg2_f, gla_bg_f, gla_wg2_b, gla_bg_b, mla_w_uq, w_o, w_gate_e, w_up_e,
                  w_down_e, w_ple, w_ple_gate):
    depth = w_in.shape[0]
    gk = GLA_HEADS * GLA_DK
    o = 0
    parts = {}
    for name, width in (("q", gk), ("k", gk), ("v", GLA_WIDTH), ("r", GLA_WIDTH), ("gf", GLA_GATE_RANK),
                        ("gb", GLA_GATE_RANK), ("cq", MLA_Q_RANK), ("ckv", MLA_KV_RANK), ("kr", MLA_ROPE)):
        parts[name] = w_in[:, :, o:o + width]
        o += width
    zeros = lambda w: jnp.zeros((depth, D_MODEL, w), w_in.dtype)
    kra = jnp.concatenate([parts["kr"], parts["gf"], parts["gb"],
                           zeros(LANES - MLA_ROPE - 2 * GLA_GATE_RANK)], axis=-1)
    krb = jnp.concatenate([_rotate_half_cols(parts["kr"]), zeros(LANES - MLA_ROPE)], axis=-1)
    w_in2 = jnp.concatenate([parts["q"], parts["k"], parts["v"], parts["r"], parts["cq"], kra, krb,
                             parts["ckv"]], axis=-1).astype(BF16)

    uq = mla_w_uq.reshape(depth, MLA_Q_RANK, MLA_HEADS, MLA_NOPE + MLA_ROPE)
    rope = uq[..., MLA_NOPE:]
    zpad = jnp.zeros(rope.shape[:-1] + (LANES - MLA_ROPE,), rope.dtype)
    wuq2 = jnp.concatenate([uq[..., :MLA_NOPE], rope, zpad, _rotate_half_cols(rope), zpad], axis=-1)
    wuq2 = wuq2.reshape(depth, MLA_Q_RANK, MLA_HEADS * MLA_Q_COLS).astype(BF16)

    wg2 = jnp.zeros((depth, LANES, 2 * gk), F32)
    wg2 = wg2.at[:, GATE_ROW_F:GATE_ROW_F + GLA_GATE_RANK, :gk].set(gla_wg2_f)
    wg2 = wg2.at[:, GATE_ROW_B:GATE_ROW_B + GLA_GATE_RANK, gk:].set(gla_wg2_b)
    bg2 = jnp.concatenate([gla_bg_f, gla_bg_b], axis=-1).reshape(depth, 1, 2 * gk)
    return dict(w_in=w_in2, wuq=wuq2, wg2=wg2, bg2=bg2,
                wo_a=w_o[:, :GLA_WIDTH].astype(BF16), wo_b=w_o[:, GLA_WIDTH:].astype(BF16),
                wge=w_gate_e.astype(BF16), wue=w_up_e.astype(BF16), wde=w_down_e.astype(BF16),
                wple=w_ple.astype(BF16), wpg=w_ple_gate.astype(BF16))


def _rope_tables(seq):
    pos = jnp.arange(seq, dtype=F32)
    inv = ROPE_THETA ** (-jnp.arange(0, MLA_ROPE, 2, dtype=F32) / MLA_ROPE)
    ang = pos[:, None] * inv[None, :]
    zpad = jnp.zeros((seq, LANES - MLA_ROPE), F32)
    cos_t = jnp.concatenate([jnp.cos(ang), jnp.cos(ang), zpad], axis=-1)
    sin_t = jnp.concatenate([jnp.sin(ang), jnp.sin(ang), zpad], axis=-1)
    return cos_t, sin_t


def _tiles(batch, seq):
    n = batch * seq
    return dict(row=min(512, seq), row_small=min(256, seq), gla=min(256, seq), tq=min(512, seq),
                tk=min(8192, seq), lst=min(256, n), moe=min(256, CAPACITY_FACTOR * n // N_EXPERTS))


def _trunk(x, p, ln_in_g, ln_in_b, pw, gla_norm_g, mla_qnorm_g, mla_kvnorm_g, mla_w_ukv, ln1_g, ln1_b,
           w_router, b_router, ln2_g, ln2_b):
    batch, seq, d = x.shape
    depth = p.shape[0]
    n = batch * seq
    cap = CAPACITY_FACTOR * n // N_EXPERTS
    alpha = (2 * depth) ** 0.25
    tl = _tiles(batch, seq)
    cos_t, sin_t = _rope_tables(seq)
    wukv = mla_w_ukv.astype(BF16)
    wr_hi = w_router.astype(BF16)
    wr_lo = (w_router - wr_hi.astype(F32)).astype(BF16)
    wr_pad = jnp.zeros((depth, d, LANES - N_EXPERTS), BF16)
    wr2 = jnp.concatenate([wr_hi, wr_pad, wr_lo, wr_pad], axis=-1)
    xf, xb = layer_norm_call(x.reshape(n, d), ln_in_g, ln_in_b, tl["row"])
    for i in range(depth):
        z = inproj_call(xb, pw["w_in"][i], tl["row"])
        mix_a = gla_call(z, pw["wg2"][i], pw["bg2"][i], gla_norm_g[i], batch, seq, tl["gla"])
        q, k, v = mla_proj_call(z, cos_t, sin_t, mla_qnorm_g[i], mla_kvnorm_g[i], pw["wuq"][i], wukv[i],
                                seq, tl["row"])
        mix_b = flash_call(q, k, v, batch, seq, tl["tq"], tl["tk"])
        h, hb, aff = outproj_call(mix_a, mix_b, xf, pw["wo_a"][i], pw["wo_b"][i], ln1_g[i], ln1_b[i],
                                  wr2[i], b_router[i], alpha, tl["row"])
        aff_t = aff.T
        thr, need = threshold_call(aff_t, cap)
        lists = list_call(thr[:, 0, 0], need[:, 0, 0], aff_t, cap, tl["lst"])
        idx = (lists[:, :cap, LIST_TID_HI] * TID_SPLIT + lists[:, :cap, LIST_TID_LO]).astype(I32).reshape(-1)
        acc = ple_call(p[i].reshape(n, PLE_DIM), hb, h, pw["wple"][i], pw["wpg"][i], alpha, tl["row_small"])
        acc = moe_call(idx, lists, pw["wge"][i], pw["wue"][i], pw["wde"][i], h, acc, cap, tl["moe"])
        xf, xb = layer_norm_call(acc, ln2_g[i], ln2_b[i], tl["row"])
    return xf.reshape(batch, seq, d)


def kernel(x_prompt, x_sample, p_prompt, p_sample, ln_in_g, ln_in_b, w_in, gla_wg2_f, gla_bg_f, gla_wg2_b, gla_bg_b, gla_norm_g, mla_qnorm_g, mla_kvnorm_g, mla_w_uq, mla_w_ukv, w_o, ln1_g, ln1_b, w_router, b_router, w_gate_e, w_up_e, w_down_e, w_ple, w_ple_gate, ln2_g, ln2_b):
    pw = _prep_weights(w_in, gla_wg2_f, gla_bg_f, gla_wg2_b, gla_bg_b, mla_w_uq, w_o, w_gate_e, w_up_e,
                       w_down_e, w_ple, w_ple_gate)
    rest = (ln_in_g, ln_in_b, pw, gla_norm_g, mla_qnorm_g, mla_kvnorm_g, mla_w_ukv, ln1_g, ln1_b,
            w_router, b_router, ln2_g, ln2_b)
    return (_trunk(x_prompt, p_prompt, *rest), _trunk(x_sample, p_sample, *rest))
```

```python
import functools
import math

import jax
import jax.numpy as jnp
from jax import lax
from jax.experimental import pallas as pl
from jax.experimental.pallas import tpu as pltpu

F32 = jnp.float32
BF16 = jnp.bfloat16
I32 = jnp.int32

D_MODEL = 2048
GLA_HEADS = 4
GLA_DK = 128
GLA_DV = 256
GLA_WIDTH = GLA_HEADS * GLA_DV
GLA_GATE_RANK = 16
GLA_NORMALIZER = 16.0
GLA_CHUNK = 64
GLA_CHUNK_SHIFT = 6
MLA_HEADS = 8
MLA_NOPE = 128
MLA_ROPE = 64
MLA_V = 128
MLA_WIDTH = MLA_HEADS * MLA_V
MLA_Q_RANK = 768
MLA_KV_RANK = 512
ROPE_THETA = 10000.0
N_EXPERTS = 16
CAPACITY_FACTOR = 2
PLE_DIM = 256
LN_EPS = 1e-5
RMS_EPS = 1e-6

LANES = 128
MLA_QK_PAD = 256
VMEM_LIMIT = 56 * 1024 * 1024

COL_Q = 0
COL_K = 512
COL_V = 1024
COL_R = 2048
COL_CQ = 3072
COL_KRA = 3840
COL_KRB = 3968
COL_CKV = 4096
IN_COLS = 4608
GATE_ROW_F = MLA_ROPE
GATE_ROW_B = MLA_ROPE + GLA_GATE_RANK


def _params(sem, vmem=VMEM_LIMIT, **kw):
    return pltpu.CompilerParams(dimension_semantics=sem, vmem_limit_bytes=vmem, **kw)


def _resident(shape, index_map):
    return pl.BlockSpec(shape, index_map, pipeline_mode=pl.Buffered(1))


def _layer_norm(y, g, b):
    mu = jnp.mean(y, axis=-1, keepdims=True)
    yc = y - mu
    var = jnp.mean(yc * yc, axis=-1, keepdims=True)
    return yc * lax.rsqrt(var + LN_EPS) * g + b


def _ln_kernel(x_ref, g_ref, b_ref, o_ref, ob_ref):
    y = _layer_norm(x_ref[...], g_ref[...], b_ref[...])
    o_ref[...] = y
    ob_ref[...] = y.astype(BF16)


def layer_norm_call(x, g, b, tm):
    n, d = x.shape
    return pl.pallas_call(
        _ln_kernel,
        grid=(n // tm,),
        in_specs=[pl.BlockSpec((tm, d), lambda i: (i, 0)),
                  pl.BlockSpec((1, d), lambda i: (0, 0)),
                  pl.BlockSpec((1, d), lambda i: (0, 0))],
        out_specs=[pl.BlockSpec((tm, d), lambda i: (i, 0)),
                   pl.BlockSpec((tm, d), lambda i: (i, 0))],
        out_shape=[jax.ShapeDtypeStruct((n, d), F32), jax.ShapeDtypeStruct((n, d), BF16)],
        compiler_params=_params(("parallel",)),
    )(x, g.reshape(1, d), b.reshape(1, d))


IN_CHUNK = 512


def _inproj_kernel(x_ref, w_ref, o_ref):
    x = x_ref[...]
    for c in range(0, IN_COLS, IN_CHUNK):
        o_ref[:, c:c + IN_CHUNK] = jnp.dot(x, w_ref[:, c:c + IN_CHUNK], preferred_element_type=F32)


def inproj_call(xb, w, tm):
    n, d = xb.shape
    return pl.pallas_call(
        _inproj_kernel,
        grid=(n // tm,),
        in_specs=[pl.BlockSpec((tm, d), lambda i: (i, 0)),
                  _resident((d, IN_COLS), lambda i: (0, 0))],
        out_specs=pl.BlockSpec((tm, IN_COLS), lambda i: (i, 0)),
        out_shape=jax.ShapeDtypeStruct((n, IN_COLS), F32),
        compiler_params=_params(("parallel",)),
    )(xb, w)


MLA_Q_COLS = 3 * LANES


def _rms(x, g):
    return x * lax.rsqrt(jnp.mean(x * x, axis=-1, keepdims=True) + RMS_EPS) * g


def _mla_proj_kernel(cq_ref, kra_ref, krb_ref, ckv_ref, cos_ref, sin_ref, gq_ref, gkv_ref,
                     wuq_ref, wukv_ref, q_ref, k_ref, v_ref):
    cos = cos_ref[...]
    sin = sin_ref[...]
    scale = (MLA_NOPE + MLA_ROPE) ** -0.5 * math.log2(math.e)
    nq = _rms(cq_ref[...], gq_ref[...]).astype(BF16)
    nkv = _rms(ckv_ref[...], gkv_ref[...]).astype(BF16)
    k_rope = (kra_ref[...] * cos + krb_ref[...] * sin).astype(BF16)
    for h in range(MLA_HEADS):
        qh = jnp.dot(nq, wuq_ref[:, h * MLA_Q_COLS:(h + 1) * MLA_Q_COLS], preferred_element_type=F32)
        q_nope = qh[:, :LANES] * scale
        q_rope = (qh[:, LANES:2 * LANES] * cos + qh[:, 2 * LANES:] * sin) * scale
        q_ref[:, h * MLA_QK_PAD:h * MLA_QK_PAD + LANES] = q_nope.astype(BF16)
        q_ref[:, h * MLA_QK_PAD + LANES:(h + 1) * MLA_QK_PAD] = q_rope.astype(BF16)
        kv = jnp.dot(nkv, wukv_ref[:, h * 2 * LANES:(h + 1) * 2 * LANES], preferred_element_type=F32)
        k_ref[:, h * MLA_QK_PAD:h * MLA_QK_PAD + LANES] = kv[:, :LANES].astype(BF16)
        k_ref[:, h * MLA_QK_PAD + LANES:(h + 1) * MLA_QK_PAD] = k_rope
        v_ref[:, h * MLA_V:(h + 1) * MLA_V] = kv[:, LANES:].astype(BF16)


def mla_proj_call(z, cos_t, sin_t, gq, gkv, wuq, wukv, seq, tm):
    n = z.shape[0]
    nseq = seq // tm
    row = lambda i: (i, 0)
    const = lambda i: (0, 0)
    return pl.pallas_call(
        _mla_proj_kernel,
        grid=(n // tm,),
        in_specs=[pl.BlockSpec((tm, MLA_Q_RANK), lambda i: (i, COL_CQ // MLA_Q_RANK)),
                  pl.BlockSpec((tm, LANES), lambda i: (i, COL_KRA // LANES)),
                  pl.BlockSpec((tm, LANES), lambda i: (i, COL_KRB // LANES)),
                  pl.BlockSpec((tm, MLA_KV_RANK), lambda i: (i, COL_CKV // MLA_KV_RANK)),
                  pl.BlockSpec((tm, LANES), lambda i: (i % nseq, 0)),
                  pl.BlockSpec((tm, LANES), lambda i: (i % nseq, 0)),
                  pl.BlockSpec((1, MLA_Q_RANK), const),
                  pl.BlockSpec((1, MLA_KV_RANK), const),
                  _resident((MLA_Q_RANK, MLA_HEADS * MLA_Q_COLS), const),
                  _resident((MLA_KV_RANK, MLA_HEADS * 2 * LANES), const)],
        out_specs=[pl.BlockSpec((tm, MLA_HEADS * MLA_QK_PAD), row),
                   pl.BlockSpec((tm, MLA_HEADS * MLA_QK_PAD), row),
                   pl.BlockSpec((tm, MLA_WIDTH), row)],
        out_shape=[jax.ShapeDtypeStruct((n, MLA_HEADS * MLA_QK_PAD), BF16),
                   jax.ShapeDtypeStruct((n, MLA_HEADS * MLA_QK_PAD), BF16),
                   jax.ShapeDtypeStruct((n, MLA_WIDTH), BF16)],
        compiler_params=_params(("parallel",)),
    )(z, z, z, z, cos_t, sin_t, gq.reshape(1, -1), gkv.reshape(1, -1), wuq, wukv)


def _flash_kernel(q_ref, k_ref, v_ref, o_ref, m_sc, l_sc, acc_sc):
    j = pl.program_id(3)

    @pl.when(j == 0)
    def _():
        m_sc[...] = jnp.full(m_sc.shape, -jnp.inf, F32)
        l_sc[...] = jnp.zeros(l_sc.shape, F32)
        acc_sc[...] = jnp.zeros(acc_sc.shape, F32)

    s = lax.dot_general(q_ref[...], k_ref[...], (((1,), (1,)), ((), ())), preferred_element_type=F32)
    m_prev = m_sc[...]
    m_new = jnp.maximum(m_prev, jnp.max(s, axis=-1, keepdims=True))
    alpha = jnp.exp2(m_prev - m_new)
    p = jnp.exp2(s - m_new)
    l_sc[...] = alpha * l_sc[...] + jnp.sum(p, axis=-1, keepdims=True)
    acc_sc[...] = alpha * acc_sc[...] + jnp.dot(p.astype(BF16), v_ref[...], preferred_element_type=F32)
    m_sc[...] = m_new

    @pl.when(j == pl.num_programs(3) - 1)
    def _():
        o_ref[...] = (acc_sc[...] / l_sc[...]).astype(o_ref.dtype)


def flash_call(q, k, v, batch, seq, tq, tk):
    n = q.shape[0]
    nq, nk = seq // tq, seq // tk
    return pl.pallas_call(
        _flash_kernel,
        grid=(batch, MLA_HEADS, nq, nk),
        in_specs=[pl.BlockSpec((tq, MLA_QK_PAD), lambda b, h, i, j: (b * nq + i, h)),
                  pl.BlockSpec((tk, MLA_QK_PAD), lambda b, h, i, j: (b * nk + j, h)),
                  pl.BlockSpec((tk, MLA_V), lambda b, h, i, j: (b * nk + j, h))],
        out_specs=pl.BlockSpec((tq, MLA_V), lambda b, h, i, j: (b * nq + i, h)),
        out_shape=jax.ShapeDtypeStruct((n, MLA_WIDTH), BF16),
        scratch_shapes=[pltpu.VMEM((tq, 1), F32), pltpu.VMEM((tq, 1), F32), pltpu.VMEM((tq, MLA_V), F32)],
        compiler_params=_params(("parallel", "parallel", "parallel", "arbitrary")),
    )(q, k, v)


def _gla_block(q, k, v, a, wg, bg, states, reverse):
    t = q.shape[0]
    c = GLA_CHUNK
    gk = GLA_HEADS * GLA_DK
    heads = range(GLA_HEADS)
    ks = [slice(h * GLA_DK, (h + 1) * GLA_DK) for h in heads]
    vs = [slice(h * GLA_DV, (h + 1) * GLA_DV) for h in heads]
    nt = (((1,), (1,)), ((), ()))
    starts = list(range(0, t, c))
    pre = jnp.dot(a, wg, preferred_element_type=F32, precision=lax.Precision.HIGHEST) + bg
    log_a = jax.nn.log_sigmoid(pre) * (1.0 / GLA_NORMALIZER)
    row = lax.broadcasted_iota(I32, (t, t), 0)
    col = lax.broadcasted_iota(I32, (t, t), 1)
    same = lax.shift_right_logical(row, GLA_CHUNK_SHIFT) == lax.shift_right_logical(col, GLA_CHUNK_SHIFT)
    if reverse:
        tri = same & (col >= row)
        keep = same & (col > row)
    else:
        tri = same & (col <= row)
        keep = same & (col <= row)
    la0 = log_a.astype(BF16)
    r1 = log_a - la0.astype(F32)
    la1 = r1.astype(BF16)
    la2 = (r1 - la1.astype(F32)).astype(BF16)
    b3 = jnp.dot(jnp.where(tri, 1.0, 0.0).astype(BF16), jnp.concatenate([la0, la1, la2], axis=1),
                 preferred_element_type=F32)
    b = b3[:, :gk] + b3[:, gk:2 * gk] + b3[:, 2 * gk:]
    ends = [b[s0:s0 + 1, :] if reverse else b[s0 + c - 1:s0 + c, :] for s0 in starts]
    b_end = jnp.concatenate([jnp.broadcast_to(e, (c, gk)) for e in ends], axis=0)
    q_in = (q * (GLA_DK ** -0.5) * jnp.exp(b)).astype(BF16)
    k_in = (k * jnp.exp(-b)).astype(BF16)
    k_end = (k * jnp.exp(b_end - b)).astype(BF16)
    decay = [jnp.exp(e) for e in ends]
    vb = v.astype(BF16)
    att = [lax.dot_general(q_in[:, ks[h]], k_in[:, ks[h]], nt, preferred_element_type=F32) for h in heads]
    att = [jnp.where(keep, x, 0.0).astype(BF16) for x in att]
    o_intra = [jnp.dot(att[h], vb[:, vs[h]], preferred_element_type=F32) for h in heads]
    upd = [[lax.dot_general(vb[s0:s0 + c, vs[h]], k_end[s0:s0 + c, ks[h]], (((0,), (0,)), ((), ())),
                            preferred_element_type=F32) for h in heads] for s0 in starts]
    outs = [[None] * len(starts) for _ in heads]
    states = list(states)
    order = reversed(range(len(starts))) if reverse else range(len(starts))
    for ci in order:
        sl = slice(starts[ci], starts[ci] + c)
        for h in heads:
            outs[h][ci] = o_intra[h][sl, :] + lax.dot_general(q_in[sl, ks[h]], states[h].astype(BF16), nt,
                                                              preferred_element_type=F32)
            states[h] = states[h] * decay[ci][:, ks[h]] + upd[ci][h]
    return [jnp.concatenate(o, axis=0) for o in outs], states


def _gla_heads(q_ref, k_ref, v_ref, a_ref, wg_ref, bg_ref, st_sc, reverse):
    @pl.when(pl.program_id(1) == 0)
    def _():
        st_sc[...] = jnp.zeros(st_sc.shape, F32)

    outs, states = _gla_block(q_ref[...], k_ref[...], v_ref[...], a_ref[...], wg_ref[...], bg_ref[...],
                              [st_sc[h] for h in range(GLA_HEADS)], reverse)
    for h, st in enumerate(states):
        st_sc[h] = st
    return outs


def _gla_fwd_kernel(q_ref, k_ref, v_ref, a_ref, wg_ref, bg_ref, o_ref, st_sc):
    outs = _gla_heads(q_ref, k_ref, v_ref, a_ref, wg_ref, bg_ref, st_sc, False)
    for h, o in enumerate(outs):
        o_ref[:, h * GLA_DV:(h + 1) * GLA_DV] = o


def _gla_bwd_kernel(q_ref, k_ref, v_ref, a_ref, wg_ref, bg_ref, of_ref, r_ref, ng_ref, o_ref, st_sc):
    outs = _gla_heads(q_ref, k_ref, v_ref, a_ref, wg_ref, bg_ref, st_sc, True)
    ng = ng_ref[...]
    for h, o in enumerate(outs):
        vv = slice(h * GLA_DV, (h + 1) * GLA_DV)
        o = _rms(o + of_ref[:, vv], ng)
        r = r_ref[:, vv]
        o_ref[:, vv] = (o * (r * jax.nn.sigmoid(r))).astype(o_ref.dtype)


def gla_call(z, wg2, bg2, norm_g, batch, seq, t):
    n = z.shape[0]
    nb = seq // t
    gk = GLA_HEADS * GLA_DK
    fwd = lambda col: (lambda b, i: (b * nb + i, col))
    rev = lambda col: (lambda b, i: (b * nb + nb - 1 - i, col))
    common = dict(grid=(batch, nb),
                  scratch_shapes=[pltpu.VMEM((GLA_HEADS, GLA_DV, GLA_DK), F32)],
                  compiler_params=_params(("parallel", "arbitrary")))

    def z_specs(at, direction):
        return [pl.BlockSpec((t, gk), at(COL_Q // gk)),
                pl.BlockSpec((t, gk), at(COL_K // gk)),
                pl.BlockSpec((t, GLA_WIDTH), at(COL_V // GLA_WIDTH)),
                pl.BlockSpec((t, LANES), at(COL_KRA // LANES)),
                pl.BlockSpec((LANES, gk), lambda b, i: (0, direction)),
                pl.BlockSpec((1, gk), lambda b, i: (0, direction))]

    o_f = pl.pallas_call(
        _gla_fwd_kernel,
        in_specs=z_specs(fwd, 0),
        out_specs=pl.BlockSpec((t, GLA_WIDTH), fwd(0)),
        out_shape=jax.ShapeDtypeStruct((n, GLA_WIDTH), F32),
        **common,
    )(z, z, z, z, wg2, bg2)
    return pl.pallas_call(
        _gla_bwd_kernel,
        in_specs=z_specs(rev, 1) + [pl.BlockSpec((t, GLA_WIDTH), rev(0)),
                                    pl.BlockSpec((t, GLA_WIDTH), rev(COL_R // GLA_WIDTH)),
                                    pl.BlockSpec((1, GLA_DV), lambda b, i: (0, 0))],
        out_specs=pl.BlockSpec((t, GLA_WIDTH), rev(0)),
        out_shape=jax.ShapeDtypeStruct((n, GLA_WIDTH), BF16),
        **common,
    )(z, z, z, z, wg2, bg2, o_f, z, norm_g.reshape(1, GLA_DV))


def _outproj_kernel(alpha, a_ref, b_ref, x_ref, woa_ref, wob_ref, g_ref, bb_ref, wr_ref, br_ref,
                    h_ref, hb_ref, aff_ref):
    mix = jnp.dot(a_ref[...], woa_ref[...], preferred_element_type=F32)
    mix = mix + jnp.dot(b_ref[...], wob_ref[...], preferred_element_type=F32)
    h = _layer_norm(alpha * x_ref[...] + mix, g_ref[...], bb_ref[...])
    h_ref[...] = h
    hb = h.astype(BF16)
    hb_ref[...] = hb
    h_lo = (h - hb.astype(F32)).astype(BF16)
    l1 = jnp.dot(hb, wr_ref[...], preferred_element_type=F32)
    l2 = jnp.dot(h_lo, wr_ref[:, :LANES], preferred_element_type=F32)
    logits = (l1[:, :LANES] + l1[:, LANES:] + l2)[:, :N_EXPERTS] + br_ref[...]
    e = jnp.exp(logits - jnp.max(logits, axis=-1, keepdims=True))
    aff_ref[...] = e / jnp.sum(e, axis=-1, keepdims=True)


def outproj_call(mix_a, mix_b, x, wo_a, wo_b, g, b, wr, br, alpha, tm):
    n, d = x.shape
    row = lambda i: (i, 0)
    const = lambda i: (0, 0)
    return pl.pallas_call(
        functools.partial(_outproj_kernel, alpha),
        grid=(n // tm,),
        in_specs=[pl.BlockSpec((tm, GLA_WIDTH), row),
                  pl.BlockSpec((tm, MLA_WIDTH), row),
                  pl.BlockSpec((tm, d), row),
                  _resident((GLA_WIDTH, d), const),
                  _resident((MLA_WIDTH, d), const),
                  pl.BlockSpec((1, d), const),
                  pl.BlockSpec((1, d), const),
                  _resident((d, 2 * LANES), const),
                  pl.BlockSpec((1, N_EXPERTS), const)],
        out_specs=[pl.BlockSpec((tm, d), row), pl.BlockSpec((tm, d), row),
                   pl.BlockSpec((tm, N_EXPERTS), row)],
        out_shape=[jax.ShapeDtypeStruct((n, d), F32), jax.ShapeDtypeStruct((n, d), BF16),
                   jax.ShapeDtypeStruct((n, N_EXPERTS), F32)],
        compiler_params=_params(("parallel",)),
    )(mix_a, mix_b, x, wo_a, wo_b, g.reshape(1, d), b.reshape(1, d), wr, br.reshape(1, N_EXPERTS))


def _threshold_kernel(cap, a_ref, thr_ref, need_ref):
    bits = pltpu.bitcast(a_ref[...], I32)

    def count(pred):
        part = jnp.sum(pred.astype(I32), axis=1, keepdims=True)
        return jnp.sum(part, axis=2, keepdims=True)

    def body(i, thr):
        cand = thr | jnp.left_shift(jnp.int32(1), 30 - i)
        return jnp.where(count(bits >= cand) >= cap, cand, thr)

    thr = lax.fori_loop(0, 31, body, jnp.zeros((N_EXPERTS, 1, 1), I32))
    thr_ref[...] = jnp.broadcast_to(thr, thr_ref.shape)
    need_ref[...] = jnp.broadcast_to(cap - count(bits > thr), need_ref.shape)


def threshold_call(aff_t, cap):
    e, n = aff_t.shape
    full = lambda: (0, 0, 0)
    return pl.pallas_call(
        functools.partial(_threshold_kernel, cap),
        in_specs=[pl.BlockSpec((e, n // LANES, LANES), full)],
        out_specs=[pl.BlockSpec((e, 1, LANES), full), pl.BlockSpec((e, 1, LANES), full)],
        out_shape=[jax.ShapeDtypeStruct((e, 1, LANES), I32), jax.ShapeDtypeStruct((e, 1, LANES), I32)],
        compiler_params=_params(None),
    )(aff_t.reshape(e, n // LANES, LANES))


LIST_TID_HI, LIST_TID_LO, LIST_G0, LIST_G1, LIST_G2 = 0, 1, 2, 3, 4
TID_SHIFT = 8
TID_SPLIT = 1 << TID_SHIFT


LIST_GROUP = 4


def _list_kernel(thr_ref, need_ref, a_ref, u_ref, o_ref):
    e0 = pl.program_id(0) * LIST_GROUP
    _, nt, t = a_ref.shape
    o_ref[...] = jnp.zeros(o_ref.shape, F32)
    u = u_ref[...]
    sub8 = lax.broadcasted_iota(I32, (8, t), 0)
    subv = lax.broadcasted_iota(I32, (LANES, t), 0)
    slot = lax.broadcasted_iota(I32, (t, t), 0)
    lane = lax.broadcasted_iota(I32, (1, t), 1)

    def one_expert(g, j, gt_before, eq_before):
        thr = thr_ref[e0 + g]
        need = need_ref[e0 + g]
        a = a_ref[g, pl.ds(j, 1), :]
        bits = pltpu.bitcast(a, I32)
        gt = bits > thr
        eq = bits == thr
        lhs = jnp.where(sub8 == 0, jnp.where(gt, 1.0, 0.0), jnp.where(sub8 == 1, jnp.where(eq, 1.0, 0.0), 0.0))
        cs = jnp.dot(lhs.astype(BF16), u, preferred_element_type=F32).astype(I32)
        c_gt = cs[0:1, :]
        c_eq = cs[1:2, :]
        tie_rank = eq_before + c_eq
        sel = gt | (eq & (tie_rank <= need))
        off = gt_before + jnp.minimum(eq_before, need)
        dest = gt_before + c_gt + jnp.minimum(tie_rank, need) - 1 - off
        dest = jnp.where(sel, dest, -1)
        onehot = (slot == dest).astype(BF16)

        tid = j * t + lane
        g0 = a.astype(BF16).astype(F32)
        g1 = (a - g0).astype(BF16).astype(F32)
        g2 = (a - g0 - g1).astype(BF16).astype(F32)
        vals = jnp.where(subv == LIST_TID_HI, lax.shift_right_logical(tid, TID_SHIFT).astype(F32),
               jnp.where(subv == LIST_TID_LO, (tid & (TID_SPLIT - 1)).astype(F32),
               jnp.where(subv == LIST_G0, g0,
               jnp.where(subv == LIST_G1, g1,
               jnp.where(subv == LIST_G2, g2, 0.0)))))
        rows = lax.dot_general(onehot, vals.astype(BF16), (((1,), (1,)), ((), ())), preferred_element_type=F32)
        win = pl.ds(off, t)
        o_ref[g, win, :] = o_ref[g, win, :] + rows
        return gt_before + jnp.sum(gt.astype(I32)), eq_before + jnp.sum(eq.astype(I32))

    def tile(j, carry):
        out = []
        for g in range(LIST_GROUP):
            out.extend(one_expert(g, j, carry[2 * g], carry[2 * g + 1]))
        return tuple(out)

    lax.fori_loop(0, nt, tile, (jnp.int32(0),) * (2 * LIST_GROUP))


def list_call(thr, need, aff_t, cap, t):
    e, n = aff_t.shape
    cpad = cap + t
    u = (jnp.arange(t)[:, None] <= jnp.arange(t)[None, :]).astype(BF16)
    return pl.pallas_call(
        _list_kernel,
        grid_spec=pltpu.PrefetchScalarGridSpec(
            num_scalar_prefetch=2,
            grid=(e // LIST_GROUP,),
            in_specs=[pl.BlockSpec((LIST_GROUP, n // t, t), lambda ei, *_: (ei, 0, 0)),
                      pl.BlockSpec((t, t), lambda ei, *_: (0, 0))],
            out_specs=pl.BlockSpec((LIST_GROUP, cpad, LANES), lambda ei, *_: (ei, 0, 0))),
        out_shape=jax.ShapeDtypeStruct((e, cpad, LANES), F32),
        compiler_params=_params(("arbitrary",)),
    )(thr, need, aff_t.reshape(e, n // t, t), u)


def _ple_kernel(alpha, p_ref, hb_ref, h_ref, wp_ref, wg_ref, o_ref):
    ple = jnp.dot(p_ref[...].astype(BF16), wp_ref[...], preferred_element_type=F32)
    gate = jnp.dot(hb_ref[...], wg_ref[...], preferred_element_type=F32)
    o_ref[...] = alpha * h_ref[...] + ple * jax.nn.sigmoid(gate)


def ple_call(p, hb, h, wp, wg, alpha, tm):
    n, d = h.shape
    row = lambda i: (i, 0)
    const = lambda i: (0, 0)
    return pl.pallas_call(
        functools.partial(_ple_kernel, alpha),
        grid=(n // tm,),
        in_specs=[pl.BlockSpec((tm, PLE_DIM), row), pl.BlockSpec((tm, d), row), pl.BlockSpec((tm, d), row),
                  _resident((PLE_DIM, d), const), _resident((d, d), const)],
        out_specs=pl.BlockSpec((tm, d), row),
        out_shape=jax.ShapeDtypeStruct((n, d), F32),
        compiler_params=_params(("parallel",)),
    )(p, hb, h, wp, wg)


def _moe_kernel(idx_ref, lst_ref, wg_ref, wu_ref, wd_ref, h_hbm, acc_in, acc_hbm,
                xbuf, xb_sc, abuf, xsem, asem, ssem):
    del acc_in
    tm = abuf.shape[0]
    nk = pl.num_programs(1)
    step = pl.program_id(0) * nk + pl.program_id(1)
    last = pl.num_programs(0) * nk - 1
    slot = step % 2

    def gather_x(s, sl):
        for r in range(tm):
            tok = idx_ref[s * tm + r]
            pltpu.make_async_copy(h_hbm.at[pl.ds(tok, 1), :], xbuf.at[sl, pl.ds(r, 1), :], xsem.at[sl]).start()

    def gather_acc(s):
        for r in range(tm):
            tok = idx_ref[s * tm + r]
            pltpu.make_async_copy(acc_hbm.at[pl.ds(tok, 1), :], abuf.at[pl.ds(r, 1), :], asem).start()

    def scatter_acc(s):
        for r in range(tm):
            tok = idx_ref[s * tm + r]
            pltpu.make_async_copy(abuf.at[pl.ds(r, 1), :], acc_hbm.at[pl.ds(tok, 1), :], ssem).start()

    def rows_done(buf, sem):
        pltpu.make_async_copy(h_hbm.at[pl.ds(0, tm), :], buf, sem).wait()

    @pl.when(step == 0)
    def _():
        gather_x(step, slot)
        gather_acc(step)
        rows_done(abuf, asem)

    rows_done(xbuf.at[slot], xsem.at[slot])
    xb_sc[...] = xbuf[slot].astype(BF16)
    gather_x(jnp.minimum(step + 1, last), 1 - slot)
    scatter_acc(jnp.maximum(step - 1, 0))
    xb = xb_sc[...]
    g = jnp.dot(xb, wg_ref[...], preferred_element_type=F32)
    u = jnp.dot(xb, wu_ref[...], preferred_element_type=F32)
    hid = (g * jax.nn.sigmoid(g) * u).astype(BF16)

    rows_done(abuf, ssem)
    gather_acc(step)
    y = jnp.dot(hid, wd_ref[...], preferred_element_type=F32)
    lst = lst_ref[...]
    gate = lst[:, LIST_G0:LIST_G0 + 1] + lst[:, LIST_G1:LIST_G1 + 1] + lst[:, LIST_G2:LIST_G2 + 1]
    y = y * gate
    rows_done(abuf, asem)
    abuf[...] = abuf[...] + y

    @pl.when(step == last)
    def _():
        scatter_acc(step)
        rows_done(abuf, ssem)
        rows_done(xbuf.at[1 - slot], xsem.at[1 - slot])


def moe_call(idx, lists, wg, wu, wd, h, acc, cap, tm):
    n, d = h.shape
    e = wg.shape[0]
    ff = wg.shape[2]
    return pl.pallas_call(
        _moe_kernel,
        grid_spec=pltpu.PrefetchScalarGridSpec(
            num_scalar_prefetch=1,
            grid=(e, cap // tm),
            in_specs=[pl.BlockSpec((None, tm, LANES), lambda ei, k, *_: (ei, k, 0)),
                      _resident((None, d, ff), lambda ei, k, *_: (ei, 0, 0)),
                      _resident((None, d, ff), lambda ei, k, *_: (ei, 0, 0)),
                      _resident((None, ff, d), lambda ei, k, *_: (ei, 0, 0)),
                      pl.BlockSpec(memory_space=pl.ANY),
                      pl.BlockSpec(memory_space=pl.ANY)],
            out_specs=pl.BlockSpec(memory_space=pl.ANY),
            scratch_shapes=[pltpu.VMEM((2, tm, d), F32), pltpu.VMEM((tm, d), BF16), pltpu.VMEM((tm, d), F32),
                            pltpu.SemaphoreType.DMA((2,)), pltpu.SemaphoreType.DMA(()),
                            pltpu.SemaphoreType.DMA(())]),
        out_shape=jax.ShapeDtypeStruct((n, d), F32),
        input_output_aliases={6: 0},
        compiler_params=_params(("arbitrary", "arbitrary"), has_side_effects=True, disable_bounds_checks=True),
    )(idx, lists, wg, wu, wd, h, acc)


def _rotate_half_cols(w):
    half = MLA_ROPE // 2
    return jnp.concatenate([-w[..., half:], w[..., :half]], axis=-1)


def _prep_weights(w_in, gla_wg2_f, gla_bg_f, gla_wg2_b, gla_bg_b, mla_w_uq, w_o, w_gate_e, w_up_e,
                  w_down_e, w_ple, w_ple_gate):
    depth = w_in.shape[0]
    gk = GLA_HEADS * GLA_DK
    o = 0
    parts = {}
    for name, width in (("q", gk), ("k", gk), ("v", GLA_WIDTH), ("r", GLA_WIDTH), ("gf", GLA_GATE_RANK),
                        ("gb", GLA_GATE_RANK), ("cq", MLA_Q_RANK), ("ckv", MLA_KV_RANK), ("kr", MLA_ROPE)):
        parts[name] = w_in[:, :, o:o + width]
        o += width
    zeros = lambda w: jnp.zeros((depth, D_MODEL, w), w_in.dtype)
    kra = jnp.concatenate([parts["kr"], parts["gf"], parts["gb"],
                           zeros(LANES - MLA_ROPE - 2 * GLA_GATE_RANK)], axis=-1)
    krb = jnp.concatenate([_rotate_half_cols(parts["kr"]), zeros(LANES - MLA_ROPE)], axis=-1)
    w_in2 = jnp.concatenate([parts["q"], parts["k"], parts["v"], parts["r"], parts["cq"], kra, krb,
                             parts["ckv"]], axis=-1).astype(BF16)

    uq = mla_w_uq.reshape(depth, MLA_Q_RANK, MLA_HEADS, MLA_NOPE + MLA_ROPE)
    rope = uq[..., MLA_NOPE:]
    zpad = jnp.zeros(rope.shape[:-1] + (LANES - MLA_ROPE,), rope.dtype)
    wuq2 = jnp.concatenate([uq[..., :MLA_NOPE], rope, zpad, _rotate_half_cols(rope), zpad], axis=-1)
    wuq2 = wuq2.reshape(depth, MLA_Q_RANK, MLA_HEADS * MLA_Q_COLS).astype(BF16)

    wg2 = jnp.zeros((depth, LANES, 2 * gk), F32)
    wg2 = wg2.at[:, GATE_ROW_F:GATE_ROW_F + GLA_GATE_RANK, :gk].set(gla_wg2_f)
    wg2 = wg2.at[:, GATE_ROW_B:GATE_ROW_B + GLA_GATE_RANK, gk:].set(gla_wg2_b)
    bg2 = jnp.concatenate([gla_bg_f, gla_bg_b], axis=-1).reshape(depth, 1, 2 * gk)
    return dict(w_in=w_in2, wuq=wuq2, wg2=wg2, bg2=bg2,
                wo_a=w_o[:, :GLA_WIDTH].astype(BF16), wo_b=w_o[:, GLA_WIDTH:].astype(BF16),
                wge=w_gate_e.astype(BF16), wue=w_up_e.astype(BF16), wde=w_down_e.astype(BF16),
                wple=w_ple.astype(BF16), wpg=w_ple_gate.astype(BF16))


def _rope_tables(seq):
    pos = jnp.arange(seq, dtype=F32)
    inv = ROPE_THETA ** (-jnp.arange(0, MLA_ROPE, 2, dtype=F32) / MLA_ROPE)
    ang = pos[:, None] * inv[None, :]
    zpad = jnp.zeros((seq, LANES - MLA_ROPE), F32)
    cos_t = jnp.concatenate([jnp.cos(ang), jnp.cos(ang), zpad], axis=-1)
    sin_t = jnp.concatenate([jnp.sin(ang), jnp.sin(ang), zpad], axis=-1)
    return cos_t, sin_t


def _tiles(batch, seq):
    n = batch * seq
    return dict(row=min(512, seq), row_small=min(256, seq), gla=min(256, seq), tq=min(512, seq),
                tk=min(8192, seq), lst=min(256, n), moe=min(256, CAPACITY_FACTOR * n // N_EXPERTS))


def _trunk(x, p, ln_in_g, ln_in_b, pw, gla_norm_g, mla_qnorm_g, mla_kvnorm_g, mla_w_ukv, ln1_g, ln1_b,
           w_router, b_router, ln2_g, ln2_b):
    batch, seq, d = x.shape
    depth = p.shape[0]
    n = batch * seq
    cap = CAPACITY_FACTOR * n // N_EXPERTS
    alpha = (2 * depth) ** 0.25
    tl = _tiles(batch, seq)
    cos_t, sin_t = _rope_tables(seq)
    wukv = mla_w_ukv.astype(BF16)
    wr_hi = w_router.astype(BF16)
    wr_lo = (w_router - wr_hi.astype(F32)).astype(BF16)
    wr_pad = jnp.zeros((depth, d, LANES - N_EXPERTS), BF16)
    wr2 = jnp.concatenate([wr_hi, wr_pad, wr_lo, wr_pad], axis=-1)
    xf, xb = layer_norm_call(x.reshape(n, d), ln_in_g, ln_in_b, tl["row"])
    for i in range(depth):
        z = inproj_call(xb, pw["w_in"][i], tl["row"])
        mix_a = gla_call(z, pw["wg2"][i], pw["bg2"][i], gla_norm_g[i], batch, seq, tl["gla"])
        q, k, v = mla_proj_call(z, cos_t, sin_t, mla_qnorm_g[i], mla_kvnorm_g[i], pw["wuq"][i], wukv[i],
                                seq, tl["row"])
        mix_b = flash_call(q, k, v, batch, seq, tl["tq"], tl["tk"])
        h, hb, aff = outproj_call(mix_a, mix_b, xf, pw["wo_a"][i], pw["wo_b"][i], ln1_g[i], ln1_b[i],
                                  wr2[i], b_router[i], alpha, tl["row"])
        aff_t = aff.T
        thr, need = threshold_call(aff_t, cap)
        lists = list_call(thr[:, 0, 0], need[:, 0, 0], aff_t, cap, tl["lst"])
        idx = (lists[:, :cap, LIST_TID_HI] * TID_SPLIT + lists[:, :cap, LIST_TID_LO]).astype(I32).reshape(-1)
        acc = ple_call(p[i].reshape(n, PLE_DIM), hb, h, pw["wple"][i], pw["wpg"][i], alpha, tl["row_small"])
        acc = moe_call(idx, lists, pw["wge"][i], pw["wue"][i], pw["wde"][i], h, acc, cap, tl["moe"])
        xf, xb = layer_norm_call(acc, ln2_g[i], ln2_b[i], tl["row"])
    return xf.reshape(batch, seq, d)


def kernel(x_prompt, x_sample, p_prompt, p_sample, ln_in_g, ln_in_b, w_in, gla_wg2_f, gla_bg_f, gla_wg2_b, gla_bg_b, gla_norm_g, mla_qnorm_g, mla_kvnorm_g, mla_w_uq, mla_w_ukv, w_o, ln1_g, ln1_b, w_router, b_router, w_gate_e, w_up_e, w_down_e, w_ple, w_ple_gate, ln2_g, ln2_b):
    pw = _prep_weights(w_in, gla_wg2_f, gla_bg_f, gla_wg2_b, gla_bg_b, mla_w_uq, w_o, w_gate_e, w_up_e,
                       w_down_e, w_ple, w_ple_gate)
    rest = (ln_in_g, ln_in_b, pw, gla_norm_g, mla_qnorm_g, mla_kvnorm_g, mla_w_ukv, ln1_g, ln1_b,
            w_router, b_router, ln2_g, ln2_b)
    return (_trunk(x_prompt, p_prompt, *rest), _trunk(x_sample, p_sample, *rest))
```

```python
import functools
import math

import jax
import jax.numpy as jnp
from jax import lax
from jax.experimental import pallas as pl
from jax.experimental.pallas import tpu as pltpu

F32 = jnp.float32
BF16 = jnp.bfloat16
I32 = jnp.int32

D_MODEL = 2048
GLA_HEADS = 4
GLA_DK = 128
GLA_DV = 256
GLA_WIDTH = GLA_HEADS * GLA_DV
GLA_GATE_RANK = 16
GLA_NORMALIZER = 16.0
GLA_CHUNK = 64
GLA_CHUNK_SHIFT = 6
MLA_HEADS = 8
MLA_NOPE = 128
MLA_ROPE = 64
MLA_V = 128
MLA_WIDTH = MLA_HEADS * MLA_V
MLA_Q_RANK = 768
MLA_KV_RANK = 512
ROPE_THETA = 10000.0
N_EXPERTS = 16
CAPACITY_FACTOR = 2
PLE_DIM = 256
LN_EPS = 1e-5
RMS_EPS = 1e-6

LANES = 128
MLA_QK_PAD = 256
VMEM_LIMIT = 56 * 1024 * 1024

COL_Q = 0
COL_K = 512
COL_V = 1024
COL_R = 2048
COL_CQ = 3072
COL_KRA = 3840
COL_KRB = 3968
COL_CKV = 4096
IN_COLS = 4608
GATE_ROW_F = MLA_ROPE
GATE_ROW_B = MLA_ROPE + GLA_GATE_RANK


def _params(sem, vmem=VMEM_LIMIT, **kw):
    return pltpu.CompilerParams(dimension_semantics=sem, vmem_limit_bytes=vmem, **kw)


def _resident(shape, index_map):
    return pl.BlockSpec(shape, index_map, pipeline_mode=pl.Buffered(1))


def _layer_norm(y, g, b):
    mu = jnp.mean(y, axis=-1, keepdims=True)
    yc = y - mu
    var = jnp.mean(yc * yc, axis=-1, keepdims=True)
    return yc * lax.rsqrt(var + LN_EPS) * g + b


def _ln_kernel(x_ref, g_ref, b_ref, o_ref, ob_ref):
    y = _layer_norm(x_ref[...], g_ref[...], b_ref[...])
    o_ref[...] = y
    ob_ref[...] = y.astype(BF16)


def layer_norm_call(x, g, b, tm):
    n, d = x.shape
    return pl.pallas_call(
        _ln_kernel,
        grid=(n // tm,),
        in_specs=[pl.BlockSpec((tm, d), lambda i: (i, 0)),
                  pl.BlockSpec((1, d), lambda i: (0, 0)),
                  pl.BlockSpec((1, d), lambda i: (0, 0))],
        out_specs=[pl.BlockSpec((tm, d), lambda i: (i, 0)),
                   pl.BlockSpec((tm, d), lambda i: (i, 0))],
        out_shape=[jax.ShapeDtypeStruct((n, d), F32), jax.ShapeDtypeStruct((n, d), BF16)],
        compiler_params=_params(("parallel",)),
    )(x, g.reshape(1, d), b.reshape(1, d))


IN_CHUNK = 512


def _inproj_kernel(x_ref, w_ref, o_ref):
    x = x_ref[...]
    for c in range(0, IN_COLS, IN_CHUNK):
        o_ref[:, c:c + IN_CHUNK] = jnp.dot(x, w_ref[:, c:c + IN_CHUNK], preferred_element_type=F32)


def inproj_call(xb, w, tm):
    n, d = xb.shape
    return pl.pallas_call(
        _inproj_kernel,
        grid=(n // tm,),
        in_specs=[pl.BlockSpec((tm, d), lambda i: (i, 0)),
                  _resident((d, IN_COLS), lambda i: (0, 0))],
        out_specs=pl.BlockSpec((tm, IN_COLS), lambda i: (i, 0)),
        out_shape=jax.ShapeDtypeStruct((n, IN_COLS), F32),
        compiler_params=_params(("parallel",)),
    )(xb, w)


MLA_Q_COLS = 3 * LANES


def _rms(x, g):
    return x * lax.rsqrt(jnp.mean(x * x, axis=-1, keepdims=True) + RMS_EPS) * g


def _mla_proj_kernel(cq_ref, kra_ref, krb_ref, ckv_ref, cos_ref, sin_ref, gq_ref, gkv_ref,
                     wuq_ref, wukv_ref, q_ref, k_ref, v_ref):
    cos = cos_ref[...]
    sin = sin_ref[...]
    scale = (MLA_NOPE + MLA_ROPE) ** -0.5 * math.log2(math.e)
    nq = _rms(cq_ref[...], gq_ref[...]).astype(BF16)
    nkv = _rms(ckv_ref[...], gkv_ref[...]).astype(BF16)
    k_rope = (kra_ref[...] * cos + krb_ref[...] * sin).astype(BF16)
    q_all = jnp.dot(nq, wuq_ref[...], preferred_element_type=F32)
    kv_all = jnp.dot(nkv, wukv_ref[...], preferred_element_type=F32)
    for h in range(MLA_HEADS):
        qh = q_all[:, h * MLA_Q_COLS:(h + 1) * MLA_Q_COLS]
        q_nope = qh[:, :LANES] * scale
        q_rope = (qh[:, LANES:2 * LANES] * cos + qh[:, 2 * LANES:] * sin) * scale
        q_ref[:, h * MLA_QK_PAD:h * MLA_QK_PAD + LANES] = q_nope.astype(BF16)
        q_ref[:, h * MLA_QK_PAD + LANES:(h + 1) * MLA_QK_PAD] = q_rope.astype(BF16)
        kv = kv_all[:, h * 2 * LANES:(h + 1) * 2 * LANES]
        k_ref[:, h * MLA_QK_PAD:h * MLA_QK_PAD + LANES] = kv[:, :LANES].astype(BF16)
        k_ref[:, h * MLA_QK_PAD + LANES:(h + 1) * MLA_QK_PAD] = k_rope
        v_ref[:, h * MLA_V:(h + 1) * MLA_V] = kv[:, LANES:].astype(BF16)


def mla_proj_call(z, cos_t, sin_t, gq, gkv, wuq, wukv, seq, tm):
    n = z.shape[0]
    nseq = seq // tm
    row = lambda i: (i, 0)
    const = lambda i: (0, 0)
    return pl.pallas_call(
        _mla_proj_kernel,
        grid=(n // tm,),
        in_specs=[pl.BlockSpec((tm, MLA_Q_RANK), lambda i: (i, COL_CQ // MLA_Q_RANK)),
                  pl.BlockSpec((tm, LANES), lambda i: (i, COL_KRA // LANES)),
                  pl.BlockSpec((tm, LANES), lambda i: (i, COL_KRB // LANES)),
                  pl.BlockSpec((tm, MLA_KV_RANK), lambda i: (i, COL_CKV // MLA_KV_RANK)),
                  pl.BlockSpec((tm, LANES), lambda i: (i % nseq, 0)),
                  pl.BlockSpec((tm, LANES), lambda i: (i % nseq, 0)),
                  pl.BlockSpec((1, MLA_Q_RANK), const),
                  pl.BlockSpec((1, MLA_KV_RANK), const),
                  _resident((MLA_Q_RANK, MLA_HEADS * MLA_Q_COLS), const),
                  _resident((MLA_KV_RANK, MLA_HEADS * 2 * LANES), const)],
        out_specs=[pl.BlockSpec((tm, MLA_HEADS * MLA_QK_PAD), row),
                   pl.BlockSpec((tm, MLA_HEADS * MLA_QK_PAD), row),
                   pl.BlockSpec((tm, MLA_WIDTH), row)],
        out_shape=[jax.ShapeDtypeStruct((n, MLA_HEADS * MLA_QK_PAD), BF16),
                   jax.ShapeDtypeStruct((n, MLA_HEADS * MLA_QK_PAD), BF16),
                   jax.ShapeDtypeStruct((n, MLA_WIDTH), BF16)],
        compiler_params=_params(("parallel",)),
    )(z, z, z, z, cos_t, sin_t, gq.reshape(1, -1), gkv.reshape(1, -1), wuq, wukv)


def _flash_kernel(q_ref, k_ref, v_ref, o_ref, m_sc, l_sc, acc_sc):
    j = pl.program_id(3)

    @pl.when(j == 0)
    def _():
        m_sc[...] = jnp.full(m_sc.shape, -jnp.inf, F32)
        l_sc[...] = jnp.zeros(l_sc.shape, F32)
        acc_sc[...] = jnp.zeros(acc_sc.shape, F32)

    s = lax.dot_general(q_ref[...], k_ref[...], (((1,), (1,)), ((), ())), preferred_element_type=F32)
    m_prev = m_sc[...]
    m_new = jnp.maximum(m_prev, jnp.max(s, axis=-1, keepdims=True))
    alpha = jnp.exp2(m_prev - m_new)
    p = jnp.exp2(s - m_new)
    l_sc[...] = alpha * l_sc[...] + jnp.sum(p, axis=-1, keepdims=True)
    acc_sc[...] = alpha * acc_sc[...] + jnp.dot(p.astype(BF16), v_ref[...], preferred_element_type=F32)
    m_sc[...] = m_new

    @pl.when(j == pl.num_programs(3) - 1)
    def _():
        o_ref[...] = (acc_sc[...] / l_sc[...]).astype(o_ref.dtype)


def flash_call(q, k, v, batch, seq, tq, tk):
    n = q.shape[0]
    nq, nk = seq // tq, seq // tk
    return pl.pallas_call(
        _flash_kernel,
        grid=(batch, MLA_HEADS, nq, nk),
        in_specs=[pl.BlockSpec((tq, MLA_QK_PAD), lambda b, h, i, j: (b * nq + i, h)),
                  pl.BlockSpec((tk, MLA_QK_PAD), lambda b, h, i, j: (b * nk + j, h)),
                  pl.BlockSpec((tk, MLA_V), lambda b, h, i, j: (b * nk + j, h))],
        out_specs=pl.BlockSpec((tq, MLA_V), lambda b, h, i, j: (b * nq + i, h)),
        out_shape=jax.ShapeDtypeStruct((n, MLA_WIDTH), BF16),
        scratch_shapes=[pltpu.VMEM((tq, 1), F32), pltpu.VMEM((tq, 1), F32), pltpu.VMEM((tq, MLA_V), F32)],
        compiler_params=_params(("parallel", "parallel", "parallel", "arbitrary")),
    )(q, k, v)


def _gla_block(q, k, v, a, wg, bg, states, reverse):
    t = q.shape[0]
    c = GLA_CHUNK
    gk = GLA_HEADS * GLA_DK
    heads = range(GLA_HEADS)
    ks = [slice(h * GLA_DK, (h + 1) * GLA_DK) for h in heads]
    vs = [slice(h * GLA_DV, (h + 1) * GLA_DV) for h in heads]
    nt = (((1,), (1,)), ((), ()))
    starts = list(range(0, t, c))
    pre = jnp.dot(a, wg, preferred_element_type=F32, precision=lax.Precision.HIGHEST) + bg
    log_a = jax.nn.log_sigmoid(pre) * (1.0 / GLA_NORMALIZER)
    row = lax.broadcasted_iota(I32, (t, t), 0)
    col = lax.broadcasted_iota(I32, (t, t), 1)
    same = lax.shift_right_logical(row, GLA_CHUNK_SHIFT) == lax.shift_right_logical(col, GLA_CHUNK_SHIFT)
    if reverse:
        tri = same & (col >= row)
        keep = same & (col > row)
    else:
        tri = same & (col <= row)
        keep = same & (col <= row)
    la0 = log_a.astype(BF16)
    r1 = log_a - la0.astype(F32)
    la1 = r1.astype(BF16)
    la2 = (r1 - la1.astype(F32)).astype(BF16)
    b3 = jnp.dot(jnp.where(tri, 1.0, 0.0).astype(BF16), jnp.concatenate([la0, la1, la2], axis=1),
                 preferred_element_type=F32)
    b = b3[:, :gk] + b3[:, gk:2 * gk] + b3[:, 2 * gk:]
    ends = [b[s0:s0 + 1, :] if reverse else b[s0 + c - 1:s0 + c, :] for s0 in starts]
    b_end = jnp.concatenate([jnp.broadcast_to(e, (c, gk)) for e in ends], axis=0)
    q_in = (q * (GLA_DK ** -0.5) * jnp.exp(b)).astype(BF16)
    k_in = (k * jnp.exp(-b)).astype(BF16)
    k_end = (k * jnp.exp(b_end - b)).astype(BF16)
    decay = [jnp.exp(e) for e in ends]
    vb = v.astype(BF16)
    att = [lax.dot_general(q_in[:, ks[h]], k_in[:, ks[h]], nt, preferred_element_type=F32) for h in heads]
    att = [jnp.where(keep, x, 0.0).astype(BF16) for x in att]
    o_intra = [jnp.dot(att[h], vb[:, vs[h]], preferred_element_type=F32) for h in heads]
    upd = [[lax.dot_general(vb[s0:s0 + c, vs[h]], k_end[s0:s0 + c, ks[h]], (((0,), (0,)), ((), ())),
                            preferred_element_type=F32) for h in heads] for s0 in starts]
    outs = [[None] * len(starts) for _ in heads]
    states = list(states)
    order = reversed(range(len(starts))) if reverse else range(len(starts))
    for ci in order:
        sl = slice(starts[ci], starts[ci] + c)
        for h in heads:
            outs[h][ci] = o_intra[h][sl, :] + lax.dot_general(q_in[sl, ks[h]], states[h].astype(BF16), nt,
                                                              preferred_element_type=F32)
            states[h] = states[h] * decay[ci][:, ks[h]] + upd[ci][h]
    return [jnp.concatenate(o, axis=0) for o in outs], states


def _gla_heads(q_ref, k_ref, v_ref, a_ref, wg_ref, bg_ref, st_sc, reverse):
    @pl.when(pl.program_id(1) == 0)
    def _():
        st_sc[...] = jnp.zeros(st_sc.shape, F32)

    outs, states = _gla_block(q_ref[...], k_ref[...], v_ref[...], a_ref[...], wg_ref[...], bg_ref[...],
                              [st_sc[h] for h in range(GLA_HEADS)], reverse)
    for h, st in enumerate(states):
        st_sc[h] = st
    return outs


def _gla_fwd_kernel(q_ref, k_ref, v_ref, a_ref, wg_ref, bg_ref, o_ref, st_sc):
    outs = _gla_heads(q_ref, k_ref, v_ref, a_ref, wg_ref, bg_ref, st_sc, False)
    for h, o in enumerate(outs):
        o_ref[:, h * GLA_DV:(h + 1) * GLA_DV] = o


def _gla_bwd_kernel(q_ref, k_ref, v_ref, a_ref, wg_ref, bg_ref, of_ref, r_ref, ng_ref, o_ref, st_sc):
    outs = _gla_heads(q_ref, k_ref, v_ref, a_ref, wg_ref, bg_ref, st_sc, True)
    ng = ng_ref[...]
    for h, o in enumerate(outs):
        vv = slice(h * GLA_DV, (h + 1) * GLA_DV)
        o = _rms(o + of_ref[:, vv], ng)
        r = r_ref[:, vv]
        o_ref[:, vv] = (o * (r * jax.nn.sigmoid(r))).astype(o_ref.dtype)


def gla_call(z, wg2, bg2, norm_g, batch, seq, t):
    n = z.shape[0]
    nb = seq // t
    gk = GLA_HEADS * GLA_DK
    fwd = lambda col: (lambda b, i: (b * nb + i, col))
    rev = lambda col: (lambda b, i: (b * nb + nb - 1 - i, col))
    common = dict(grid=(batch, nb),
                  scratch_shapes=[pltpu.VMEM((GLA_HEADS, GLA_DV, GLA_DK), F32)],
                  compiler_params=_params(("parallel", "arbitrary")))

    def z_specs(at, direction):
        return [pl.BlockSpec((t, gk), at(COL_Q // gk)),
                pl.BlockSpec((t, gk), at(COL_K // gk)),
                pl.BlockSpec((t, GLA_WIDTH), at(COL_V // GLA_WIDTH)),
                pl.BlockSpec((t, LANES), at(COL_KRA // LANES)),
                pl.BlockSpec((LANES, gk), lambda b, i: (0, direction)),
                pl.BlockSpec((1, gk), lambda b, i: (0, direction))]

    o_f = pl.pallas_call(
        _gla_fwd_kernel,
        in_specs=z_specs(fwd, 0),
        out_specs=pl.BlockSpec((t, GLA_WIDTH), fwd(0)),
        out_shape=jax.ShapeDtypeStruct((n, GLA_WIDTH), F32),
        **common,
    )(z, z, z, z, wg2, bg2)
    return pl.pallas_call(
        _gla_bwd_kernel,
        in_specs=z_specs(rev, 1) + [pl.BlockSpec((t, GLA_WIDTH), rev(0)),
                                    pl.BlockSpec((t, GLA_WIDTH), rev(COL_R // GLA_WIDTH)),
                                    pl.BlockSpec((1, GLA_DV), lambda b, i: (0, 0))],
        out_specs=pl.BlockSpec((t, GLA_WIDTH), rev(0)),
        out_shape=jax.ShapeDtypeStruct((n, GLA_WIDTH), BF16),
        **common,
    )(z, z, z, z, wg2, bg2, o_f, z, norm_g.reshape(1, GLA_DV))


def _outproj_kernel(alpha, a_ref, b_ref, x_ref, woa_ref, wob_ref, g_ref, bb_ref, wr_ref, br_ref,
                    h_ref, hb_ref, aff_ref):
    mix = jnp.dot(a_ref[...], woa_ref[...], preferred_element_type=F32)
    mix = mix + jnp.dot(b_ref[...], wob_ref[...], preferred_element_type=F32)
    h = _layer_norm(alpha * x_ref[...] + mix, g_ref[...], bb_ref[...])
    h_ref[...] = h
    hb = h.astype(BF16)
    hb_ref[...] = hb
    h_lo = (h - hb.astype(F32)).astype(BF16)
    l1 = jnp.dot(hb, wr_ref[...], preferred_element_type=F32)
    l2 = jnp.dot(h_lo, wr_ref[:, :LANES], preferred_element_type=F32)
    logits = (l1[:, :LANES] + l1[:, LANES:] + l2)[:, :N_EXPERTS] + br_ref[...]
    e = jnp.exp(logits - jnp.max(logits, axis=-1, keepdims=True))
    aff_ref[...] = e / jnp.sum(e, axis=-1, keepdims=True)


def outproj_call(mix_a, mix_b, x, wo_a, wo_b, g, b, wr, br, alpha, tm):
    n, d = x.shape
    row = lambda i: (i, 0)
    const = lambda i: (0, 0)
    return pl.pallas_call(
        functools.partial(_outproj_kernel, alpha),
        grid=(n // tm,),
        in_specs=[pl.BlockSpec((tm, GLA_WIDTH), row),
                  pl.BlockSpec((tm, MLA_WIDTH), row),
                  pl.BlockSpec((tm, d), row),
                  _resident((GLA_WIDTH, d), const),
                  _resident((MLA_WIDTH, d), const),
                  pl.BlockSpec((1, d), const),
                  pl.BlockSpec((1, d), const),
                  _resident((d, 2 * LANES), const),
                  pl.BlockSpec((1, N_EXPERTS), const)],
        out_specs=[pl.BlockSpec((tm, d), row), pl.BlockSpec((tm, d), row),
                   pl.BlockSpec((tm, N_EXPERTS), row)],
        out_shape=[jax.ShapeDtypeStruct((n, d), F32), jax.ShapeDtypeStruct((n, d), BF16),
                   jax.ShapeDtypeStruct((n, N_EXPERTS), F32)],
        compiler_params=_params(("parallel",)),
    )(mix_a, mix_b, x, wo_a, wo_b, g.reshape(1, d), b.reshape(1, d), wr, br.reshape(1, N_EXPERTS))


def _threshold_kernel(cap, a_ref, thr_ref, need_ref):
    bits = pltpu.bitcast(a_ref[...], I32)

    def count(pred):
        part = jnp.sum(pred.astype(I32), axis=1, keepdims=True)
        return jnp.sum(part, axis=2, keepdims=True)

    def body(i, thr):
        cand = thr | jnp.left_shift(jnp.int32(1), 30 - i)
        return jnp.where(count(bits >= cand) >= cap, cand, thr)

    thr = lax.fori_loop(0, 31, body, jnp.zeros((N_EXPERTS, 1, 1), I32))
    thr_ref[...] = jnp.broadcast_to(thr, thr_ref.shape)
    need_ref[...] = jnp.broadcast_to(cap - count(bits > thr), need_ref.shape)


def threshold_call(aff_t, cap):
    e, n = aff_t.shape
    full = lambda: (0, 0, 0)
    return pl.pallas_call(
        functools.partial(_threshold_kernel, cap),
        in_specs=[pl.BlockSpec((e, n // LANES, LANES), full)],
        out_specs=[pl.BlockSpec((e, 1, LANES), full), pl.BlockSpec((e, 1, LANES), full)],
        out_shape=[jax.ShapeDtypeStruct((e, 1, LANES), I32), jax.ShapeDtypeStruct((e, 1, LANES), I32)],
        compiler_params=_params(None),
    )(aff_t.reshape(e, n // LANES, LANES))


LIST_TID_HI, LIST_TID_LO, LIST_G0, LIST_G1, LIST_G2 = 0, 1, 2, 3, 4
TID_SHIFT = 8
TID_SPLIT = 1 << TID_SHIFT


LIST_GROUP = 4


def _list_kernel(thr_ref, need_ref, a_ref, u_ref, o_ref):
    e0 = pl.program_id(0) * LIST_GROUP
    _, nt, t = a_ref.shape
    o_ref[...] = jnp.zeros(o_ref.shape, F32)
    u = u_ref[...]
    sub8 = lax.broadcasted_iota(I32, (8, t), 0)
    subv = lax.broadcasted_iota(I32, (LANES, t), 0)
    slot = lax.broadcasted_iota(I32, (t, t), 0)
    lane = lax.broadcasted_iota(I32, (1, t), 1)

    def one_expert(g, j, gt_before, eq_before):
        thr = thr_ref[e0 + g]
        need = need_ref[e0 + g]
        a = a_ref[g, pl.ds(j, 1), :]
        bits = pltpu.bitcast(a, I32)
        gt = bits > thr
        eq = bits == thr
        lhs = jnp.where(sub8 == 0, jnp.where(gt, 1.0, 0.0), jnp.where(sub8 == 1, jnp.where(eq, 1.0, 0.0), 0.0))
        cs = jnp.dot(lhs.astype(BF16), u, preferred_element_type=F32).astype(I32)
        c_gt = cs[0:1, :]
        c_eq = cs[1:2, :]
        tie_rank = eq_before + c_eq
        sel = gt | (eq & (tie_rank <= need))
        off = gt_before + jnp.minimum(eq_before, need)
        dest = gt_before + c_gt + jnp.minimum(tie_rank, need) - 1 - off
        dest = jnp.where(sel, dest, -1)
        onehot = (slot == dest).astype(BF16)

        tid = j * t + lane
        g0 = a.astype(BF16).astype(F32)
        g1 = (a - g0).astype(BF16).astype(F32)
        g2 = (a - g0 - g1).astype(BF16).astype(F32)
        vals = jnp.where(subv == LIST_TID_HI, lax.shift_right_logical(tid, TID_SHIFT).astype(F32),
               jnp.where(subv == LIST_TID_LO, (tid & (TID_SPLIT - 1)).astype(F32),
               jnp.where(subv == LIST_G0, g0,
               jnp.where(subv == LIST_G1, g1,
               jnp.where(subv == LIST_G2, g2, 0.0)))))
        rows = lax.dot_general(onehot, vals.astype(BF16), (((1,), (1,)), ((), ())), preferred_element_type=F32)
        win = pl.ds(off, t)
        o_ref[g, win, :] = o_ref[g, win, :] + rows
        return gt_before + jnp.sum(gt.astype(I32)), eq_before + jnp.sum(eq.astype(I32))

    def tile(j, carry):
        out = []
        for g in range(LIST_GROUP):
            out.extend(one_expert(g, j, carry[2 * g], carry[2 * g + 1]))
        return tuple(out)

    lax.fori_loop(0, nt, tile, (jnp.int32(0),) * (2 * LIST_GROUP))


def list_call(thr, need, aff_t, cap, t):
    e, n = aff_t.shape
    cpad = cap + t
    u = (jnp.arange(t)[:, None] <= jnp.arange(t)[None, :]).astype(BF16)
    return pl.pallas_call(
        _list_kernel,
        grid_spec=pltpu.PrefetchScalarGridSpec(
            num_scalar_prefetch=2,
            grid=(e // LIST_GROUP,),
            in_specs=[pl.BlockSpec((LIST_GROUP, n // t, t), lambda ei, *_: (ei, 0, 0)),
                      pl.BlockSpec((t, t), lambda ei, *_: (0, 0))],
            out_specs=pl.BlockSpec((LIST_GROUP, cpad, LANES), lambda ei, *_: (ei, 0, 0))),
        out_shape=jax.ShapeDtypeStruct((e, cpad, LANES), F32),
        compiler_params=_params(("arbitrary",)),
    )(thr, need, aff_t.reshape(e, n // t, t), u)


def _ple_kernel(alpha, p_ref, hb_ref, h_ref, wp_ref, wg_ref, o_ref):
    ple = jnp.dot(p_ref[...].astype(BF16), wp_ref[...], preferred_element_type=F32)
    gate = jnp.dot(hb_ref[...], wg_ref[...], preferred_element_type=F32)
    o_ref[...] = alpha * h_ref[...] + ple * jax.nn.sigmoid(gate)


def ple_call(p, hb, h, wp, wg, alpha, tm):
    n, d = h.shape
    row = lambda i: (i, 0)
    const = lambda i: (0, 0)
    return pl.pallas_call(
        functools.partial(_ple_kernel, alpha),
        grid=(n // tm,),
        in_specs=[pl.BlockSpec((tm, PLE_DIM), row), pl.BlockSpec((tm, d), row), pl.BlockSpec((tm, d), row),
                  _resident((PLE_DIM, d), const), _resident((d, d), const)],
        out_specs=pl.BlockSpec((tm, d), row),
        out_shape=jax.ShapeDtypeStruct((n, d), F32),
        compiler_params=_params(("parallel",)),
    )(p, hb, h, wp, wg)


def _moe_kernel(idx_ref, lst_ref, wg_ref, wu_ref, wd_ref, h_hbm, acc_in, acc_hbm,
                xbuf, xb_sc, abuf, xsem, asem, ssem):
    del acc_in
    tm = abuf.shape[0]
    nk = pl.num_programs(1)
    step = pl.program_id(0) * nk + pl.program_id(1)
    last = pl.num_programs(0) * nk - 1
    slot = step % 2

    def gather_x(s, sl):
        for r in range(tm):
            tok = idx_ref[s * tm + r]
            pltpu.make_async_copy(h_hbm.at[pl.ds(tok, 1), :], xbuf.at[sl, pl.ds(r, 1), :], xsem.at[sl]).start()

    def gather_acc(s):
        for r in range(tm):
            tok = idx_ref[s * tm + r]
            pltpu.make_async_copy(acc_hbm.at[pl.ds(tok, 1), :], abuf.at[pl.ds(r, 1), :], asem).start()

    def scatter_acc(s):
        for r in range(tm):
            tok = idx_ref[s * tm + r]
            pltpu.make_async_copy(abuf.at[pl.ds(r, 1), :], acc_hbm.at[pl.ds(tok, 1), :], ssem).start()

    def rows_done(buf, sem):
        pltpu.make_async_copy(h_hbm.at[pl.ds(0, tm), :], buf, sem).wait()

    @pl.when(step == 0)
    def _():
        gather_x(step, slot)
        gather_acc(step)
        rows_done(abuf, asem)

    rows_done(xbuf.at[slot], xsem.at[slot])
    xb_sc[...] = xbuf[slot].astype(BF16)
    gather_x(jnp.minimum(step + 1, last), 1 - slot)
    scatter_acc(jnp.maximum(step - 1, 0))
    xb = xb_sc[...]
    g = jnp.dot(xb, wg_ref[...], preferred_element_type=F32)
    u = jnp.dot(xb, wu_ref[...], preferred_element_type=F32)
    hid = (g * jax.nn.sigmoid(g) * u).astype(BF16)

    rows_done(abuf, ssem)
    gather_acc(step)
    y = jnp.dot(hid, wd_ref[...], preferred_element_type=F32)
    lst = lst_ref[...]
    gate = lst[:, LIST_G0:LIST_G0 + 1] + lst[:, LIST_G1:LIST_G1 + 1] + lst[:, LIST_G2:LIST_G2 + 1]
    y = y * gate
    rows_done(abuf, asem)
    abuf[...] = abuf[...] + y

    @pl.when(step == last)
    def _():
        scatter_acc(step)
        rows_done(abuf, ssem)
        rows_done(xbuf.at[1 - slot], xsem.at[1 - slot])


def moe_call(idx, lists, wg, wu, wd, h, acc, cap, tm):
    n, d = h.shape
    e = wg.shape[0]
    ff = wg.shape[2]
    return pl.pallas_call(
        _moe_kernel,
        grid_spec=pltpu.PrefetchScalarGridSpec(
            num_scalar_prefetch=1,
            grid=(e, cap // tm),
            in_specs=[pl.BlockSpec((None, tm, LANES), lambda ei, k, *_: (ei, k, 0)),
                      _resident((None, d, ff), lambda ei, k, *_: (ei, 0, 0)),
                      _resident((None, d, ff), lambda ei, k, *_: (ei, 0, 0)),
                      _resident((None, ff, d), lambda ei, k, *_: (ei, 0, 0)),
                      pl.BlockSpec(memory_space=pl.ANY),
                      pl.BlockSpec(memory_space=pl.ANY)],
            out_specs=pl.BlockSpec(memory_space=pl.ANY),
            scratch_shapes=[pltpu.VMEM((2, tm, d), F32), pltpu.VMEM((tm, d), BF16), pltpu.VMEM((tm, d), F32),
                            pltpu.SemaphoreType.DMA((2,)), pltpu.SemaphoreType.DMA(()),
                            pltpu.SemaphoreType.DMA(())]),
        out_shape=jax.ShapeDtypeStruct((n, d), F32),
        input_output_aliases={6: 0},
        compiler_params=_params(("arbitrary", "arbitrary"), has_side_effects=True, disable_bounds_checks=True),
    )(idx, lists, wg, wu, wd, h, acc)


def _rotate_half_cols(w):
    half = MLA_ROPE // 2
    return jnp.concatenate([-w[..., half:], w[..., :half]], axis=-1)


def _prep_weights(w_in, gla_wg2_f, gla_bg_f, gla_wg2_b, gla_bg_b, mla_w_uq, w_o, w_gate_e, w_up_e,
                  w_down_e, w_ple, w_ple_gate):
    depth = w_in.shape[0]
    gk = GLA_HEADS * GLA_DK
    o = 0
    parts = {}
    for name, width in (("q", gk), ("k", gk), ("v", GLA_WIDTH), ("r", GLA_WIDTH), ("gf", GLA_GATE_RANK),
                        ("gb", GLA_GATE_RANK), ("cq", MLA_Q_RANK), ("ckv", MLA_KV_RANK), ("kr", MLA_ROPE)):
        parts[name] = w_in[:, :, o:o + width]
        o += width
    zeros = lambda w: jnp.zeros((depth, D_MODEL, w), w_in.dtype)
    kra = jnp.concatenate([parts["kr"], parts["gf"], parts["gb"],
                           zeros(LANES - MLA_ROPE - 2 * GLA_GATE_RANK)], axis=-1)
    krb = jnp.concatenate([_rotate_half_cols(parts["kr"]), zeros(LANES - MLA_ROPE)], axis=-1)
    w_in2 = jnp.concatenate([parts["q"], parts["k"], parts["v"], parts["r"], parts["cq"], kra, krb,
                             parts["ckv"]], axis=-1).astype(BF16)

    uq = mla_w_uq.reshape(depth, MLA_Q_RANK, MLA_HEADS, MLA_NOPE + MLA_ROPE)
    rope = uq[..., MLA_NOPE:]
    zpad = jnp.zeros(rope.shape[:-1] + (LANES - MLA_ROPE,), rope.dtype)
    wuq2 = jnp.concatenate([uq[..., :MLA_NOPE], rope, zpad, _rotate_half_cols(rope), zpad], axis=-1)
    wuq2 = wuq2.reshape(depth, MLA_Q_RANK, MLA_HEADS * MLA_Q_COLS).astype(BF16)

    wg2 = jnp.zeros((depth, LANES, 2 * gk), F32)
    wg2 = wg2.at[:, GATE_ROW_F:GATE_ROW_F + GLA_GATE_RANK, :gk].set(gla_wg2_f)
    wg2 = wg2.at[:, GATE_ROW_B:GATE_ROW_B + GLA_GATE_RANK, gk:].set(gla_wg2_b)
    bg2 = jnp.concatenate([gla_bg_f, gla_bg_b], axis=-1).reshape(depth, 1, 2 * gk)
    per_layer = lambda w: [w[i].astype(BF16) for i in range(depth)]
    return dict(w_in=w_in2, wuq=wuq2, wg2=wg2, bg2=bg2,
                wo_a=per_layer(w_o[:, :GLA_WIDTH]), wo_b=per_layer(w_o[:, GLA_WIDTH:]),
                wge=per_layer(w_gate_e), wue=per_layer(w_up_e), wde=per_layer(w_down_e),
                wple=per_layer(w_ple), wpg=per_layer(w_ple_gate))


def _rope_tables(seq):
    pos = jnp.arange(seq, dtype=F32)
    inv = ROPE_THETA ** (-jnp.arange(0, MLA_ROPE, 2, dtype=F32) / MLA_ROPE)
    ang = pos[:, None] * inv[None, :]
    zpad = jnp.zeros((seq, LANES - MLA_ROPE), F32)
    cos_t = jnp.concatenate([jnp.cos(ang), jnp.cos(ang), zpad], axis=-1)
    sin_t = jnp.concatenate([jnp.sin(ang), jnp.sin(ang), zpad], axis=-1)
    return cos_t, sin_t


def _tiles(batch, seq):
    n = batch * seq
    return dict(row=min(512, seq), gla=min(256, seq), tq=min(1024, seq),
                tk=min(8192, seq), lst=min(256, n), moe=min(256, CAPACITY_FACTOR * n // N_EXPERTS))


def _trunk(x, p, ln_in_g, ln_in_b, pw, gla_norm_g, mla_qnorm_g, mla_kvnorm_g, mla_w_ukv, ln1_g, ln1_b,
           w_router, b_router, ln2_g, ln2_b):
    batch, seq, d = x.shape
    depth = p.shape[0]
    n = batch * seq
    cap = CAPACITY_FACTOR * n // N_EXPERTS
    alpha = (2 * depth) ** 0.25
    tl = _tiles(batch, seq)
    cos_t, sin_t = _rope_tables(seq)
    wukv = mla_w_ukv.astype(BF16)
    wr_hi = w_router.astype(BF16)
    wr_lo = (w_router - wr_hi.astype(F32)).astype(BF16)
    wr_pad = jnp.zeros((depth, d, LANES - N_EXPERTS), BF16)
    wr2 = jnp.concatenate([wr_hi, wr_pad, wr_lo, wr_pad], axis=-1)
    xf, xb = layer_norm_call(x.reshape(n, d), ln_in_g, ln_in_b, tl["row"])
    for i in range(depth):
        z = inproj_call(xb, pw["w_in"][i], tl["row"])
        mix_a = gla_call(z, pw["wg2"][i], pw["bg2"][i], gla_norm_g[i], batch, seq, tl["gla"])
        q, k, v = mla_proj_call(z, cos_t, sin_t, mla_qnorm_g[i], mla_kvnorm_g[i], pw["wuq"][i], wukv[i],
                                seq, tl["row"])
        mix_b = flash_call(q, k, v, batch, seq, tl["tq"], tl["tk"])
        h, hb, aff = outproj_call(mix_a, mix_b, xf, pw["wo_a"][i], pw["wo_b"][i], ln1_g[i], ln1_b[i],
                                  wr2[i], b_router[i], alpha, tl["row"])
        aff_t = aff.T
        thr, need = threshold_call(aff_t, cap)
        lists = list_call(thr[:, 0, 0], need[:, 0, 0], aff_t, cap, tl["lst"])
        idx = (lists[:, :cap, LIST_TID_HI] * TID_SPLIT + lists[:, :cap, LIST_TID_LO]).astype(I32).reshape(-1)
        acc = ple_call(p[i].reshape(n, PLE_DIM), hb, h, pw["wple"][i], pw["wpg"][i], alpha, tl["row"])
        acc = moe_call(idx, lists, pw["wge"][i], pw["wue"][i], pw["wde"][i], h, acc, cap, tl["moe"])
        xf, xb = layer_norm_call(acc, ln2_g[i], ln2_b[i], tl["row"])
    return xf.reshape(batch, seq, d)


def kernel(x_prompt, x_sample, p_prompt, p_sample, ln_in_g, ln_in_b, w_in, gla_wg2_f, gla_bg_f, gla_wg2_b, gla_bg_b, gla_norm_g, mla_qnorm_g, mla_kvnorm_g, mla_w_uq, mla_w_ukv, w_o, ln1_g, ln1_b, w_router, b_router, w_gate_e, w_up_e, w_down_e, w_ple, w_ple_gate, ln2_g, ln2_b):
    pw = _prep_weights(w_in, gla_wg2_f, gla_bg_f, gla_wg2_b, gla_bg_b, mla_w_uq, w_o, w_gate_e, w_up_e,
                       w_down_e, w_ple, w_ple_gate)
    rest = (ln_in_g, ln_in_b, pw, gla_norm_g, mla_qnorm_g, mla_kvnorm_g, mla_w_ukv, ln1_g, ln1_b,
            w_router, b_router, ln2_g, ln2_b)
    return (_trunk(x_prompt, p_prompt, *rest), _trunk(x_sample, p_sample, *rest))
```

```python
import functools
import math

import jax
import jax.numpy as jnp
from jax import lax
from jax.experimental import pallas as pl
from jax.experimental.pallas import tpu as pltpu

F32 = jnp.float32
BF16 = jnp.bfloat16
I32 = jnp.int32

D_MODEL = 2048
GLA_HEADS = 4
GLA_DK = 128
GLA_DV = 256
GLA_WIDTH = GLA_HEADS * GLA_DV
GLA_GATE_RANK = 16
GLA_NORMALIZER = 16.0
GLA_CHUNK = 64
GLA_CHUNK_SHIFT = 6
MLA_HEADS = 8
MLA_NOPE = 128
MLA_ROPE = 64
MLA_V = 128
MLA_WIDTH = MLA_HEADS * MLA_V
MLA_Q_RANK = 768
MLA_KV_RANK = 512
ROPE_THETA = 10000.0
N_EXPERTS = 16
CAPACITY_FACTOR = 2
PLE_DIM = 256
LN_EPS = 1e-5
RMS_EPS = 1e-6

LANES = 128
MLA_QK_PAD = 256
VMEM_LIMIT = 56 * 1024 * 1024

COL_Q = 0
COL_K = 512
COL_V = 1024
COL_R = 2048
COL_CQ = 3072
COL_KRA = 3840
COL_KRB = 3968
COL_CKV = 4096
IN_COLS = 4608
GATE_ROW_F = MLA_ROPE
GATE_ROW_B = MLA_ROPE + GLA_GATE_RANK


def _params(sem, vmem=VMEM_LIMIT, **kw):
    return pltpu.CompilerParams(dimension_semantics=sem, vmem_limit_bytes=vmem, **kw)


def _resident(shape, index_map):
    return pl.BlockSpec(shape, index_map, pipeline_mode=pl.Buffered(1))


def _layer_norm(y, g, b):
    mu = jnp.mean(y, axis=-1, keepdims=True)
    yc = y - mu
    var = jnp.mean(yc * yc, axis=-1, keepdims=True)
    return yc * lax.rsqrt(var + LN_EPS) * g + b


def _ln_kernel(x_ref, g_ref, b_ref, o_ref, ob_ref):
    y = _layer_norm(x_ref[...], g_ref[...], b_ref[...])
    o_ref[...] = y
    ob_ref[...] = y.astype(BF16)


def layer_norm_call(x, g, b, tm):
    n, d = x.shape
    return pl.pallas_call(
        _ln_kernel,
        grid=(n // tm,),
        in_specs=[pl.BlockSpec((tm, d), lambda i: (i, 0)),
                  pl.BlockSpec((1, d), lambda i: (0, 0)),
                  pl.BlockSpec((1, d), lambda i: (0, 0))],
        out_specs=[pl.BlockSpec((tm, d), lambda i: (i, 0)),
                   pl.BlockSpec((tm, d), lambda i: (i, 0))],
        out_shape=[jax.ShapeDtypeStruct((n, d), F32), jax.ShapeDtypeStruct((n, d), BF16)],
        compiler_params=_params(("parallel",)),
    )(x, g.reshape(1, d), b.reshape(1, d))


IN_CHUNK = 512


def _inproj_kernel(x_ref, w_ref, o_ref):
    x = x_ref[...]
    for c in range(0, IN_COLS, IN_CHUNK):
        o_ref[:, c:c + IN_CHUNK] = jnp.dot(x, w_ref[:, c:c + IN_CHUNK], preferred_element_type=F32)


def inproj_call(xb, w, tm):
    n, d = xb.shape
    return pl.pallas_call(
        _inproj_kernel,
        grid=(n // tm,),
        in_specs=[pl.BlockSpec((tm, d), lambda i: (i, 0)),
                  _resident((d, IN_COLS), lambda i: (0, 0))],
        out_specs=pl.BlockSpec((tm, IN_COLS), lambda i: (i, 0)),
        out_shape=jax.ShapeDtypeStruct((n, IN_COLS), F32),
        compiler_params=_params(("parallel",)),
    )(xb, w)


MLA_Q_COLS = 3 * LANES


def _rms(x, g):
    return x * lax.rsqrt(jnp.mean(x * x, axis=-1, keepdims=True) + RMS_EPS) * g


def _mla_proj_kernel(cq_ref, kra_ref, krb_ref, ckv_ref, cos_ref, sin_ref, gq_ref, gkv_ref,
                     wuq_ref, wukv_ref, q_ref, k_ref, v_ref):
    cos = cos_ref[...]
    sin = sin_ref[...]
    scale = (MLA_NOPE + MLA_ROPE) ** -0.5 * math.log2(math.e)
    nq = _rms(cq_ref[...], gq_ref[...]).astype(BF16)
    nkv = _rms(ckv_ref[...], gkv_ref[...]).astype(BF16)
    k_rope = (kra_ref[...] * cos + krb_ref[...] * sin).astype(BF16)
    q_all = jnp.dot(nq, wuq_ref[...], preferred_element_type=F32)
    kv_all = jnp.dot(nkv, wukv_ref[...], preferred_element_type=F32)
    for h in range(MLA_HEADS):
        qh = q_all[:, h * MLA_Q_COLS:(h + 1) * MLA_Q_COLS]
        q_nope = qh[:, :LANES] * scale
        q_rope = (qh[:, LANES:2 * LANES] * cos + qh[:, 2 * LANES:] * sin) * scale
        q_ref[:, h * MLA_QK_PAD:h * MLA_QK_PAD + LANES] = q_nope.astype(BF16)
        q_ref[:, h * MLA_QK_PAD + LANES:(h + 1) * MLA_QK_PAD] = q_rope.astype(BF16)
        kv = kv_all[:, h * 2 * LANES:(h + 1) * 2 * LANES]
        k_ref[:, h * MLA_QK_PAD:h * MLA_QK_PAD + LANES] = kv[:, :LANES].astype(BF16)
        k_ref[:, h * MLA_QK_PAD + LANES:(h + 1) * MLA_QK_PAD] = k_rope
        v_ref[:, h * MLA_V:(h + 1) * MLA_V] = kv[:, LANES:].astype(BF16)


def mla_proj_call(z, cos_t, sin_t, gq, gkv, wuq, wukv, seq, tm):
    n = z.shape[0]
    nseq = seq // tm
    row = lambda i: (i, 0)
    const = lambda i: (0, 0)
    return pl.pallas_call(
        _mla_proj_kernel,
        grid=(n // tm,),
        in_specs=[pl.BlockSpec((tm, MLA_Q_RANK), lambda i: (i, COL_CQ // MLA_Q_RANK)),
                  pl.BlockSpec((tm, LANES), lambda i: (i, COL_KRA // LANES)),
                  pl.BlockSpec((tm, LANES), lambda i: (i, COL_KRB // LANES)),
                  pl.BlockSpec((tm, MLA_KV_RANK), lambda i: (i, COL_CKV // MLA_KV_RANK)),
                  pl.BlockSpec((tm, LANES), lambda i: (i % nseq, 0)),
                  pl.BlockSpec((tm, LANES), lambda i: (i % nseq, 0)),
                  pl.BlockSpec((1, MLA_Q_RANK), const),
                  pl.BlockSpec((1, MLA_KV_RANK), const),
                  _resident((MLA_Q_RANK, MLA_HEADS * MLA_Q_COLS), const),
                  _resident((MLA_KV_RANK, MLA_HEADS * 2 * LANES), const)],
        out_specs=[pl.BlockSpec((tm, MLA_HEADS * MLA_QK_PAD), row),
                   pl.BlockSpec((tm, MLA_HEADS * MLA_QK_PAD), row),
                   pl.BlockSpec((tm, MLA_WIDTH), row)],
        out_shape=[jax.ShapeDtypeStruct((n, MLA_HEADS * MLA_QK_PAD), BF16),
                   jax.ShapeDtypeStruct((n, MLA_HEADS * MLA_QK_PAD), BF16),
                   jax.ShapeDtypeStruct((n, MLA_WIDTH), BF16)],
        compiler_params=_params(("parallel",)),
    )(z, z, z, z, cos_t, sin_t, gq.reshape(1, -1), gkv.reshape(1, -1), wuq, wukv)


def _flash_kernel(q_ref, k_ref, v_ref, o_ref, m_sc, l_sc, acc_sc):
    j = pl.program_id(3)

    @pl.when(j == 0)
    def _():
        m_sc[...] = jnp.full(m_sc.shape, -jnp.inf, F32)
        l_sc[...] = jnp.zeros(l_sc.shape, F32)
        acc_sc[...] = jnp.zeros(acc_sc.shape, F32)

    s = lax.dot_general(q_ref[...], k_ref[...], (((1,), (1,)), ((), ())), preferred_element_type=F32)
    m_prev = m_sc[...]
    m_new = jnp.maximum(m_prev, jnp.max(s, axis=-1, keepdims=True))
    alpha = jnp.exp2(m_prev - m_new)
    p = jnp.exp2(s - m_new)
    l_sc[...] = alpha * l_sc[...] + jnp.sum(p, axis=-1, keepdims=True)
    acc_sc[...] = alpha * acc_sc[...] + jnp.dot(p.astype(BF16), v_ref[...], preferred_element_type=F32)
    m_sc[...] = m_new

    @pl.when(j == pl.num_programs(3) - 1)
    def _():
        o_ref[...] = (acc_sc[...] / l_sc[...]).astype(o_ref.dtype)


def flash_call(q, k, v, batch, seq, tq, tk):
    n = q.shape[0]
    nq, nk = seq // tq, seq // tk
    return pl.pallas_call(
        _flash_kernel,
        grid=(batch, MLA_HEADS, nq, nk),
        in_specs=[pl.BlockSpec((tq, MLA_QK_PAD), lambda b, h, i, j: (b * nq + i, h)),
                  pl.BlockSpec((tk, MLA_QK_PAD), lambda b, h, i, j: (b * nk + j, h)),
                  pl.BlockSpec((tk, MLA_V), lambda b, h, i, j: (b * nk + j, h))],
        out_specs=pl.BlockSpec((tq, MLA_V), lambda b, h, i, j: (b * nq + i, h)),
        out_shape=jax.ShapeDtypeStruct((n, MLA_WIDTH), BF16),
        scratch_shapes=[pltpu.VMEM((tq, 1), F32), pltpu.VMEM((tq, 1), F32), pltpu.VMEM((tq, MLA_V), F32)],
        compiler_params=_params(("parallel", "parallel", "parallel", "arbitrary")),
    )(q, k, v)


def _gla_block(q, k, v, a, wg, bg, states, reverse):
    t = q.shape[0]
    c = GLA_CHUNK
    gk = GLA_HEADS * GLA_DK
    heads = range(GLA_HEADS)
    ks = [slice(h * GLA_DK, (h + 1) * GLA_DK) for h in heads]
    vs = [slice(h * GLA_DV, (h + 1) * GLA_DV) for h in heads]
    nt = (((1,), (1,)), ((), ()))
    starts = list(range(0, t, c))
    pre = jnp.dot(a, wg, preferred_element_type=F32, precision=lax.Precision.HIGHEST) + bg
    log_a = jax.nn.log_sigmoid(pre) * (1.0 / GLA_NORMALIZER)
    row = lax.broadcasted_iota(I32, (t, t), 0)
    col = lax.broadcasted_iota(I32, (t, t), 1)
    same = lax.shift_right_logical(row, GLA_CHUNK_SHIFT) == lax.shift_right_logical(col, GLA_CHUNK_SHIFT)
    if reverse:
        tri = same & (col >= row)
        keep = same & (col > row)
    else:
        tri = same & (col <= row)
        keep = same & (col <= row)
    la0 = log_a.astype(BF16)
    r1 = log_a - la0.astype(F32)
    la1 = r1.astype(BF16)
    la2 = (r1 - la1.astype(F32)).astype(BF16)
    b3 = jnp.dot(jnp.where(tri, 1.0, 0.0).astype(BF16), jnp.concatenate([la0, la1, la2], axis=1),
                 preferred_element_type=F32)
    b = b3[:, :gk] + b3[:, gk:2 * gk] + b3[:, 2 * gk:]
    ends = [b[s0:s0 + 1, :] if reverse else b[s0 + c - 1:s0 + c, :] for s0 in starts]
    b_end = jnp.concatenate([jnp.broadcast_to(e, (c, gk)) for e in ends], axis=0)
    q_in = (q * (GLA_DK ** -0.5) * jnp.exp(b)).astype(BF16)
    k_in = (k * jnp.exp(-b)).astype(BF16)
    k_end = (k * jnp.exp(b_end - b)).astype(BF16)
    decay = [jnp.exp(e) for e in ends]
    vb = v.astype(BF16)
    att = [lax.dot_general(q_in[:, ks[h]], k_in[:, ks[h]], nt, preferred_element_type=F32) for h in heads]
    att = [jnp.where(keep, x, 0.0).astype(BF16) for x in att]
    o_intra = [jnp.dot(att[h], vb[:, vs[h]], preferred_element_type=F32) for h in heads]
    upd = [[lax.dot_general(vb[s0:s0 + c, vs[h]], k_end[s0:s0 + c, ks[h]], (((0,), (0,)), ((), ())),
                            preferred_element_type=F32) for h in heads] for s0 in starts]
    outs = [[None] * len(starts) for _ in heads]
    states = list(states)
    order = reversed(range(len(starts))) if reverse else range(len(starts))
    for ci in order:
        sl = slice(starts[ci], starts[ci] + c)
        for h in heads:
            outs[h][ci] = o_intra[h][sl, :] + lax.dot_general(q_in[sl, ks[h]], states[h].astype(BF16), nt,
                                                              preferred_element_type=F32)
            states[h] = states[h] * decay[ci][:, ks[h]] + upd[ci][h]
    return [jnp.concatenate(o, axis=0) for o in outs], states


def _gla_heads(q_ref, k_ref, v_ref, a_ref, wg_ref, bg_ref, st_sc, reverse):
    @pl.when(pl.program_id(1) == 0)
    def _():
        st_sc[...] = jnp.zeros(st_sc.shape, F32)

    outs, states = _gla_block(q_ref[...], k_ref[...], v_ref[...], a_ref[...], wg_ref[...], bg_ref[...],
                              [st_sc[h] for h in range(GLA_HEADS)], reverse)
    for h, st in enumerate(states):
        st_sc[h] = st
    return outs


def _gla_fwd_kernel(q_ref, k_ref, v_ref, a_ref, wg_ref, bg_ref, o_ref, st_sc):
    outs = _gla_heads(q_ref, k_ref, v_ref, a_ref, wg_ref, bg_ref, st_sc, False)
    for h, o in enumerate(outs):
        o_ref[:, h * GLA_DV:(h + 1) * GLA_DV] = o


def _gla_bwd_kernel(q_ref, k_ref, v_ref, a_ref, wg_ref, bg_ref, of_ref, r_ref, ng_ref, o_ref, st_sc):
    outs = _gla_heads(q_ref, k_ref, v_ref, a_ref, wg_ref, bg_ref, st_sc, True)
    ng = ng_ref[...]
    for h, o in enumerate(outs):
        vv = slice(h * GLA_DV, (h + 1) * GLA_DV)
        o = _rms(o + of_ref[:, vv], ng)
        r = r_ref[:, vv]
        o_ref[:, vv] = (o * (r * jax.nn.sigmoid(r))).astype(o_ref.dtype)


def gla_call(z, wg2, bg2, norm_g, batch, seq, t):
    n = z.shape[0]
    nb = seq // t
    gk = GLA_HEADS * GLA_DK
    fwd = lambda col: (lambda b, i: (b * nb + i, col))
    rev = lambda col: (lambda b, i: (b * nb + nb - 1 - i, col))
    common = dict(grid=(batch, nb),
                  scratch_shapes=[pltpu.VMEM((GLA_HEADS, GLA_DV, GLA_DK), F32)],
                  compiler_params=_params(("parallel", "arbitrary")))

    def z_specs(at, direction):
        return [pl.BlockSpec((t, gk), at(COL_Q // gk)),
                pl.BlockSpec((t, gk), at(COL_K // gk)),
                pl.BlockSpec((t, GLA_WIDTH), at(COL_V // GLA_WIDTH)),
                pl.BlockSpec((t, LANES), at(COL_KRA // LANES)),
                pl.BlockSpec((LANES, gk), lambda b, i: (0, direction)),
                pl.BlockSpec((1, gk), lambda b, i: (0, direction))]

    o_f = pl.pallas_call(
        _gla_fwd_kernel,
        in_specs=z_specs(fwd, 0),
        out_specs=pl.BlockSpec((t, GLA_WIDTH), fwd(0)),
        out_shape=jax.ShapeDtypeStruct((n, GLA_WIDTH), F32),
        **common,
    )(z, z, z, z, wg2, bg2)
    return pl.pallas_call(
        _gla_bwd_kernel,
        in_specs=z_specs(rev, 1) + [pl.BlockSpec((t, GLA_WIDTH), rev(0)),
                                    pl.BlockSpec((t, GLA_WIDTH), rev(COL_R // GLA_WIDTH)),
                                    pl.BlockSpec((1, GLA_DV), lambda b, i: (0, 0))],
        out_specs=pl.BlockSpec((t, GLA_WIDTH), rev(0)),
        out_shape=jax.ShapeDtypeStruct((n, GLA_WIDTH), BF16),
        **common,
    )(z, z, z, z, wg2, bg2, o_f, z, norm_g.reshape(1, GLA_DV))


def _outproj_kernel(alpha, a_ref, b_ref, x_ref, woa_ref, wob_ref, g_ref, bb_ref, wr_ref, br_ref,
                    h_ref, hb_ref, aff_ref):
    mix = jnp.dot(a_ref[...], woa_ref[...], preferred_element_type=F32)
    mix = mix + jnp.dot(b_ref[...], wob_ref[...], preferred_element_type=F32)
    h = _layer_norm(alpha * x_ref[...] + mix, g_ref[...], bb_ref[...])
    h_ref[...] = h
    hb = h.astype(BF16)
    hb_ref[...] = hb
    h_lo = (h - hb.astype(F32)).astype(BF16)
    l1 = jnp.dot(hb, wr_ref[...], preferred_element_type=F32)
    l2 = jnp.dot(h_lo, wr_ref[:, :LANES], preferred_element_type=F32)
    logits = (l1[:, :LANES] + l1[:, LANES:] + l2)[:, :N_EXPERTS] + br_ref[...]
    e = jnp.exp(logits - jnp.max(logits, axis=-1, keepdims=True))
    aff_ref[...] = e / jnp.sum(e, axis=-1, keepdims=True)


def outproj_call(mix_a, mix_b, x, wo_a, wo_b, g, b, wr, br, alpha, tm):
    n, d = x.shape
    row = lambda i: (i, 0)
    const = lambda i: (0, 0)
    return pl.pallas_call(
        functools.partial(_outproj_kernel, alpha),
        grid=(n // tm,),
        in_specs=[pl.BlockSpec((tm, GLA_WIDTH), row),
                  pl.BlockSpec((tm, MLA_WIDTH), row),
                  pl.BlockSpec((tm, d), row),
                  _resident((GLA_WIDTH, d), const),
                  _resident((MLA_WIDTH, d), const),
                  pl.BlockSpec((1, d), const),
                  pl.BlockSpec((1, d), const),
                  _resident((d, 2 * LANES), const),
                  pl.BlockSpec((1, N_EXPERTS), const)],
        out_specs=[pl.BlockSpec((tm, d), row), pl.BlockSpec((tm, d), row),
                   pl.BlockSpec((tm, N_EXPERTS), row)],
        out_shape=[jax.ShapeDtypeStruct((n, d), F32), jax.ShapeDtypeStruct((n, d), BF16),
                   jax.ShapeDtypeStruct((n, N_EXPERTS), F32)],
        compiler_params=_params(("parallel",)),
    )(mix_a, mix_b, x, wo_a, wo_b, g.reshape(1, d), b.reshape(1, d), wr, br.reshape(1, N_EXPERTS))


def _threshold_kernel(cap, a_ref, thr_ref, need_ref):
    bits = pltpu.bitcast(a_ref[...], I32)

    def count(pred):
        part = jnp.sum(pred.astype(I32), axis=1, keepdims=True)
        return jnp.sum(part, axis=2, keepdims=True)

    def body(i, thr):
        cand = thr | jnp.left_shift(jnp.int32(1), 30 - i)
        return jnp.where(count(bits >= cand) >= cap, cand, thr)

    thr = lax.fori_loop(0, 31, body, jnp.zeros((N_EXPERTS, 1, 1), I32))
    thr_ref[...] = jnp.broadcast_to(thr, thr_ref.shape)
    need_ref[...] = jnp.broadcast_to(cap - count(bits > thr), need_ref.shape)


def threshold_call(aff_t, cap):
    e, n = aff_t.shape
    full = lambda: (0, 0, 0)
    return pl.pallas_call(
        functools.partial(_threshold_kernel, cap),
        in_specs=[pl.BlockSpec((e, n // LANES, LANES), full)],
        out_specs=[pl.BlockSpec((e, 1, LANES), full), pl.BlockSpec((e, 1, LANES), full)],
        out_shape=[jax.ShapeDtypeStruct((e, 1, LANES), I32), jax.ShapeDtypeStruct((e, 1, LANES), I32)],
        compiler_params=_params(None),
    )(aff_t.reshape(e, n // LANES, LANES))


LIST_TID_HI, LIST_TID_LO, LIST_G0, LIST_G1, LIST_G2 = 0, 1, 2, 3, 4
TID_SHIFT = 8
TID_SPLIT = 1 << TID_SHIFT


LIST_GROUP = 4


def _list_kernel(thr_ref, need_ref, a_ref, u_ref, o_ref):
    e0 = pl.program_id(0) * LIST_GROUP
    _, nt, t = a_ref.shape
    o_ref[...] = jnp.zeros(o_ref.shape, F32)
    u = u_ref[...]
    sub8 = lax.broadcasted_iota(I32, (8, t), 0)
    subv = lax.broadcasted_iota(I32, (LANES, t), 0)
    slot = lax.broadcasted_iota(I32, (t, t), 0)
    lane = lax.broadcasted_iota(I32, (1, t), 1)

    def one_expert(g, j, gt_before, eq_before):
        thr = thr_ref[e0 + g]
        need = need_ref[e0 + g]
        a = a_ref[g, pl.ds(j, 1), :]
        bits = pltpu.bitcast(a, I32)
        gt = bits > thr
        eq = bits == thr
        lhs = jnp.where(sub8 == 0, jnp.where(gt, 1.0, 0.0), jnp.where(sub8 == 1, jnp.where(eq, 1.0, 0.0), 0.0))
        cs = jnp.dot(lhs.astype(BF16), u, preferred_element_type=F32).astype(I32)
        c_gt = cs[0:1, :]
        c_eq = cs[1:2, :]
        tie_rank = eq_before + c_eq
        sel = gt | (eq & (tie_rank <= need))
        off = gt_before + jnp.minimum(eq_before, need)
        dest = gt_before + c_gt + jnp.minimum(tie_rank, need) - 1 - off
        dest = jnp.where(sel, dest, -1)
        onehot = (slot == dest).astype(BF16)

        tid = j * t + lane
        g0 = a.astype(BF16).astype(F32)
        g1 = (a - g0).astype(BF16).astype(F32)
        g2 = (a - g0 - g1).astype(BF16).astype(F32)
        vals = jnp.where(subv == LIST_TID_HI, lax.shift_right_logical(tid, TID_SHIFT).astype(F32),
               jnp.where(subv == LIST_TID_LO, (tid & (TID_SPLIT - 1)).astype(F32),
               jnp.where(subv == LIST_G0, g0,
               jnp.where(subv == LIST_G1, g1,
               jnp.where(subv == LIST_G2, g2, 0.0)))))
        rows = lax.dot_general(onehot, vals.astype(BF16), (((1,), (1,)), ((), ())), preferred_element_type=F32)
        win = pl.ds(off, t)
        o_ref[g, win, :] = o_ref[g, win, :] + rows
        return gt_before + jnp.sum(gt.astype(I32)), eq_before + jnp.sum(eq.astype(I32))

    def tile(j, carry):
        out = []
        for g in range(LIST_GROUP):
            out.extend(one_expert(g, j, carry[2 * g], carry[2 * g + 1]))
        return tuple(out)

    lax.fori_loop(0, nt, tile, (jnp.int32(0),) * (2 * LIST_GROUP))


def list_call(thr, need, aff_t, cap, t):
    e, n = aff_t.shape
    cpad = cap + t
    u = (jnp.arange(t)[:, None] <= jnp.arange(t)[None, :]).astype(BF16)
    return pl.pallas_call(
        _list_kernel,
        grid_spec=pltpu.PrefetchScalarGridSpec(
            num_scalar_prefetch=2,
            grid=(e // LIST_GROUP,),
            in_specs=[pl.BlockSpec((LIST_GROUP, n // t, t), lambda ei, *_: (ei, 0, 0)),
                      pl.BlockSpec((t, t), lambda ei, *_: (0, 0))],
            out_specs=pl.BlockSpec((LIST_GROUP, cpad, LANES), lambda ei, *_: (ei, 0, 0))),
        out_shape=jax.ShapeDtypeStruct((e, cpad, LANES), F32),
        compiler_params=_params(("arbitrary",)),
    )(thr, need, aff_t.reshape(e, n // t, t), u)


def _ple_kernel(alpha, p_ref, hb_ref, h_ref, wp_ref, wg_ref, o_ref):
    ple = jnp.dot(p_ref[...].astype(BF16), wp_ref[...], preferred_element_type=F32)
    gate = jnp.dot(hb_ref[...], wg_ref[...], preferred_element_type=F32)
    o_ref[...] = alpha * h_ref[...] + ple * jax.nn.sigmoid(gate)


def ple_call(p, hb, h, wp, wg, alpha, tm, layer):
    n, d = h.shape
    row = lambda i: (i, 0)
    const = lambda i: (0, 0)
    return pl.pallas_call(
        functools.partial(_ple_kernel, alpha),
        grid=(n // tm,),
        in_specs=[pl.BlockSpec((None, tm, PLE_DIM), lambda i: (layer, i, 0)),
                  pl.BlockSpec((tm, d), row), pl.BlockSpec((tm, d), row),
                  _resident((PLE_DIM, d), const), _resident((d, d), const)],
        out_specs=pl.BlockSpec((tm, d), row),
        out_shape=jax.ShapeDtypeStruct((n, d), F32),
        compiler_params=_params(("parallel",)),
    )(p, hb, h, wp, wg)


def _moe_kernel(idx_ref, lst_ref, wg_ref, wu_ref, wd_ref, h_hbm, acc_in, acc_hbm,
                xbuf, xb_sc, abuf, xsem, asem, ssem):
    del acc_in
    tm = abuf.shape[0]
    nk = pl.num_programs(1)
    step = pl.program_id(0) * nk + pl.program_id(1)
    last = pl.num_programs(0) * nk - 1
    slot = step % 2

    def gather_x(s, sl):
        for r in range(tm):
            tok = idx_ref[s * tm + r]
            pltpu.make_async_copy(h_hbm.at[pl.ds(tok, 1), :], xbuf.at[sl, pl.ds(r, 1), :], xsem.at[sl]).start()

    def gather_acc(s):
        for r in range(tm):
            tok = idx_ref[s * tm + r]
            pltpu.make_async_copy(acc_hbm.at[pl.ds(tok, 1), :], abuf.at[pl.ds(r, 1), :], asem).start()

    def scatter_acc(s):
        for r in range(tm):
            tok = idx_ref[s * tm + r]
            pltpu.make_async_copy(abuf.at[pl.ds(r, 1), :], acc_hbm.at[pl.ds(tok, 1), :], ssem).start()

    def rows_done(buf, sem):
        pltpu.make_async_copy(h_hbm.at[pl.ds(0, tm), :], buf, sem).wait()

    @pl.when(step == 0)
    def _():
        gather_x(step, slot)
        gather_acc(step)
        rows_done(abuf, asem)

    rows_done(xbuf.at[slot], xsem.at[slot])
    xb_sc[...] = xbuf[slot].astype(BF16)
    gather_x(jnp.minimum(step + 1, last), 1 - slot)
    scatter_acc(jnp.maximum(step - 1, 0))
    xb = xb_sc[...]
    g = jnp.dot(xb, wg_ref[...], preferred_element_type=F32)
    u = jnp.dot(xb, wu_ref[...], preferred_element_type=F32)
    hid = (g * jax.nn.sigmoid(g) * u).astype(BF16)

    rows_done(abuf, ssem)
    gather_acc(step)
    y = jnp.dot(hid, wd_ref[...], preferred_element_type=F32)
    lst = lst_ref[...]
    gate = lst[:, LIST_G0:LIST_G0 + 1] + lst[:, LIST_G1:LIST_G1 + 1] + lst[:, LIST_G2:LIST_G2 + 1]
    y = y * gate
    rows_done(abuf, asem)
    abuf[...] = abuf[...] + y

    @pl.when(step == last)
    def _():
        scatter_acc(step)
        rows_done(abuf, ssem)
        rows_done(xbuf.at[1 - slot], xsem.at[1 - slot])


def moe_call(idx, lists, wg, wu, wd, h, acc, cap, tm, layer):
    n, d = h.shape
    e = wg.shape[1]
    ff = wg.shape[3]
    expert = lambda ei, k, *_: (layer, ei, 0, 0)
    return pl.pallas_call(
        _moe_kernel,
        grid_spec=pltpu.PrefetchScalarGridSpec(
            num_scalar_prefetch=1,
            grid=(e, cap // tm),
            in_specs=[pl.BlockSpec((None, tm, LANES), lambda ei, k, *_: (ei, k, 0)),
                      _resident((None, None, d, ff), expert),
                      _resident((None, None, d, ff), expert),
                      _resident((None, None, ff, d), expert),
                      pl.BlockSpec(memory_space=pl.ANY),
                      pl.BlockSpec(memory_space=pl.ANY)],
            out_specs=pl.BlockSpec(memory_space=pl.ANY),
            scratch_shapes=[pltpu.VMEM((2, tm, d), F32), pltpu.VMEM((tm, d), BF16), pltpu.VMEM((tm, d), F32),
                            pltpu.SemaphoreType.DMA((2,)), pltpu.SemaphoreType.DMA(()),
                            pltpu.SemaphoreType.DMA(())]),
        out_shape=jax.ShapeDtypeStruct((n, d), F32),
        input_output_aliases={6: 0},
        compiler_params=_params(("arbitrary", "arbitrary"), has_side_effects=True, disable_bounds_checks=True),
    )(idx, lists, wg, wu, wd, h, acc)


def _rotate_half_cols(w):
    half = MLA_ROPE // 2
    return jnp.concatenate([-w[..., half:], w[..., :half]], axis=-1)


def _prep_weights(w_in, gla_wg2_f, gla_bg_f, gla_wg2_b, gla_bg_b, mla_w_uq, w_o, w_gate_e, w_up_e,
                  w_down_e, w_ple, w_ple_gate):
    depth = w_in.shape[0]
    gk = GLA_HEADS * GLA_DK
    o = 0
    parts = {}
    for name, width in (("q", gk), ("k", gk), ("v", GLA_WIDTH), ("r", GLA_WIDTH), ("gf", GLA_GATE_RANK),
                        ("gb", GLA_GATE_RANK), ("cq", MLA_Q_RANK), ("ckv", MLA_KV_RANK), ("kr", MLA_ROPE)):
        parts[name] = w_in[:, :, o:o + width]
        o += width
    zeros = lambda w: jnp.zeros((depth, D_MODEL, w), w_in.dtype)
    kra = jnp.concatenate([parts["kr"], parts["gf"], parts["gb"],
                           zeros(LANES - MLA_ROPE - 2 * GLA_GATE_RANK)], axis=-1)
    krb = jnp.concatenate([_rotate_half_cols(parts["kr"]), zeros(LANES - MLA_ROPE)], axis=-1)
    w_in2 = jnp.concatenate([parts["q"], parts["k"], parts["v"], parts["r"], parts["cq"], kra, krb,
                             parts["ckv"]], axis=-1).astype(BF16)

    uq = mla_w_uq.reshape(depth, MLA_Q_RANK, MLA_HEADS, MLA_NOPE + MLA_ROPE)
    rope = uq[..., MLA_NOPE:]
    zpad = jnp.zeros(rope.shape[:-1] + (LANES - MLA_ROPE,), rope.dtype)
    wuq2 = jnp.concatenate([uq[..., :MLA_NOPE], rope, zpad, _rotate_half_cols(rope), zpad], axis=-1)
    wuq2 = wuq2.reshape(depth, MLA_Q_RANK, MLA_HEADS * MLA_Q_COLS).astype(BF16)

    wg2 = jnp.zeros((depth, LANES, 2 * gk), F32)
    wg2 = wg2.at[:, GATE_ROW_F:GATE_ROW_F + GLA_GATE_RANK, :gk].set(gla_wg2_f)
    wg2 = wg2.at[:, GATE_ROW_B:GATE_ROW_B + GLA_GATE_RANK, gk:].set(gla_wg2_b)
    bg2 = jnp.concatenate([gla_bg_f, gla_bg_b], axis=-1).reshape(depth, 1, 2 * gk)
    per_layer = lambda w: [w[i].astype(BF16) for i in range(depth)]
    return dict(w_in=w_in2, wuq=wuq2, wg2=wg2, bg2=bg2,
                wo_a=per_layer(w_o[:, :GLA_WIDTH]), wo_b=per_layer(w_o[:, GLA_WIDTH:]),
                wge=w_gate_e.astype(BF16), wue=w_up_e.astype(BF16), wde=w_down_e.astype(BF16),
                wple=per_layer(w_ple), wpg=per_layer(w_ple_gate))


def _rope_tables(seq):
    pos = jnp.arange(seq, dtype=F32)
    inv = ROPE_THETA ** (-jnp.arange(0, MLA_ROPE, 2, dtype=F32) / MLA_ROPE)
    ang = pos[:, None] * inv[None, :]
    zpad = jnp.zeros((seq, LANES - MLA_ROPE), F32)
    cos_t = jnp.concatenate([jnp.cos(ang), jnp.cos(ang), zpad], axis=-1)
    sin_t = jnp.concatenate([jnp.sin(ang), jnp.sin(ang), zpad], axis=-1)
    return cos_t, sin_t


def _tiles(batch, seq):
    n = batch * seq
    return dict(row=min(512, seq), gla=min(256, seq), tq=min(1024, seq),
                tk=min(8192, seq), lst=min(256, n), moe=min(256, CAPACITY_FACTOR * n // N_EXPERTS))


def _trunk(x, p, ln_in_g, ln_in_b, pw, gla_norm_g, mla_qnorm_g, mla_kvnorm_g, mla_w_ukv, ln1_g, ln1_b,
           w_router, b_router, ln2_g, ln2_b):
    batch, seq, d = x.shape
    depth = p.shape[0]
    n = batch * seq
    cap = CAPACITY_FACTOR * n // N_EXPERTS
    alpha = (2 * depth) ** 0.25
    tl = _tiles(batch, seq)
    cos_t, sin_t = _rope_tables(seq)
    wukv = mla_w_ukv.astype(BF16)
    wr_hi = w_router.astype(BF16)
    wr_lo = (w_router - wr_hi.astype(F32)).astype(BF16)
    wr_pad = jnp.zeros((depth, d, LANES - N_EXPERTS), BF16)
    wr2 = jnp.concatenate([wr_hi, wr_pad, wr_lo, wr_pad], axis=-1)
    xf, xb = layer_norm_call(x.reshape(n, d), ln_in_g, ln_in_b, tl["row"])
    for i in range(depth):
        z = inproj_call(xb, pw["w_in"][i], tl["row"])
        mix_a = gla_call(z, pw["wg2"][i], pw["bg2"][i], gla_norm_g[i], batch, seq, tl["gla"])
        q, k, v = mla_proj_call(z, cos_t, sin_t, mla_qnorm_g[i], mla_kvnorm_g[i], pw["wuq"][i], wukv[i],
                                seq, tl["row"])
        mix_b = flash_call(q, k, v, batch, seq, tl["tq"], tl["tk"])
        h, hb, aff = outproj_call(mix_a, mix_b, xf, pw["wo_a"][i], pw["wo_b"][i], ln1_g[i], ln1_b[i],
                                  wr2[i], b_router[i], alpha, tl["row"])
        aff_t = aff.T
        thr, need = threshold_call(aff_t, cap)
        lists = list_call(thr[:, 0, 0], need[:, 0, 0], aff_t, cap, tl["lst"])
        idx = (lists[:, :cap, LIST_TID_HI] * TID_SPLIT + lists[:, :cap, LIST_TID_LO]).astype(I32).reshape(-1)
        acc = ple_call(p.reshape(depth, n, PLE_DIM), hb, h, pw["wple"][i], pw["wpg"][i], alpha, tl["row"], i)
        acc = moe_call(idx, lists, pw["wge"], pw["wue"], pw["wde"], h, acc, cap, tl["moe"], i)
        xf, xb = layer_norm_call(acc, ln2_g[i], ln2_b[i], tl["row"])
    return xf.reshape(batch, seq, d)


def kernel(x_prompt, x_sample, p_prompt, p_sample, ln_in_g, ln_in_b, w_in, gla_wg2_f, gla_bg_f, gla_wg2_b, gla_bg_b, gla_norm_g, mla_qnorm_g, mla_kvnorm_g, mla_w_uq, mla_w_ukv, w_o, ln1_g, ln1_b, w_router, b_router, w_gate_e, w_up_e, w_down_e, w_ple, w_ple_gate, ln2_g, ln2_b):
    pw = _prep_weights(w_in, gla_wg2_f, gla_bg_f, gla_wg2_b, gla_bg_b, mla_w_uq, w_o, w_gate_e, w_up_e,
                       w_down_e, w_ple, w_ple_gate)
    rest = (ln_in_g, ln_in_b, pw, gla_norm_g, mla_qnorm_g, mla_kvnorm_g, mla_w_ukv, ln1_g, ln1_b,
            w_router, b_router, ln2_g, ln2_b)
    return (_trunk(x_prompt, p_prompt, *rest), _trunk(x_sample, p_sample, *rest))
```

```python
import functools
import math

import jax
import jax.numpy as jnp
from jax import lax
from jax.experimental import pallas as pl
from jax.experimental.pallas import tpu as pltpu

F32 = jnp.float32
BF16 = jnp.bfloat16
I32 = jnp.int32

D_MODEL = 2048
GLA_HEADS = 4
GLA_DK = 128
GLA_DV = 256
GLA_WIDTH = GLA_HEADS * GLA_DV
GLA_GATE_RANK = 16
GLA_NORMALIZER = 16.0
GLA_CHUNK = 64
GLA_CHUNK_SHIFT = 6
MLA_HEADS = 8
MLA_NOPE = 128
MLA_ROPE = 64
MLA_V = 128
MLA_WIDTH = MLA_HEADS * MLA_V
MLA_Q_RANK = 768
MLA_KV_RANK = 512
ROPE_THETA = 10000.0
N_EXPERTS = 16
CAPACITY_FACTOR = 2
PLE_DIM = 256
LN_EPS = 1e-5
RMS_EPS = 1e-6

LANES = 128
MLA_QK_PAD = 256
VMEM_LIMIT = 56 * 1024 * 1024

COL_Q = 0
COL_K = 512
COL_V = 1024
COL_R = 2048
COL_CQ = 3072
COL_KRA = 3840
COL_KRB = 3968
COL_CKV = 4096
IN_COLS = 4608
GATE_ROW_F = MLA_ROPE
GATE_ROW_B = MLA_ROPE + GLA_GATE_RANK


def _params(sem, vmem=VMEM_LIMIT, **kw):
    return pltpu.CompilerParams(dimension_semantics=sem, vmem_limit_bytes=vmem, **kw)


def _resident(shape, index_map):
    return pl.BlockSpec(shape, index_map, pipeline_mode=pl.Buffered(1))


def _layer_norm(y, g, b):
    mu = jnp.mean(y, axis=-1, keepdims=True)
    yc = y - mu
    var = jnp.mean(yc * yc, axis=-1, keepdims=True)
    return yc * lax.rsqrt(var + LN_EPS) * g + b


def _ln_kernel(x_ref, g_ref, b_ref, o_ref, ob_ref):
    y = _layer_norm(x_ref[...], g_ref[...], b_ref[...])
    o_ref[...] = y
    ob_ref[...] = y.astype(BF16)


def layer_norm_call(x, g, b, tm):
    n, d = x.shape
    return pl.pallas_call(
        _ln_kernel,
        grid=(n // tm,),
        in_specs=[pl.BlockSpec((tm, d), lambda i: (i, 0)),
                  pl.BlockSpec((1, d), lambda i: (0, 0)),
                  pl.BlockSpec((1, d), lambda i: (0, 0))],
        out_specs=[pl.BlockSpec((tm, d), lambda i: (i, 0)),
                   pl.BlockSpec((tm, d), lambda i: (i, 0))],
        out_shape=[jax.ShapeDtypeStruct((n, d), F32), jax.ShapeDtypeStruct((n, d), BF16)],
        compiler_params=_params(("parallel",)),
    )(x, g.reshape(1, d), b.reshape(1, d))


IN_CHUNK = 512


def _inproj_kernel(x_ref, w_ref, o_ref):
    x = x_ref[...]
    for c in range(0, IN_COLS, IN_CHUNK):
        o_ref[:, c:c + IN_CHUNK] = jnp.dot(x, w_ref[:, c:c + IN_CHUNK], preferred_element_type=F32)


def inproj_call(xb, w, tm):
    n, d = xb.shape
    return pl.pallas_call(
        _inproj_kernel,
        grid=(n // tm,),
        in_specs=[pl.BlockSpec((tm, d), lambda i: (i, 0)),
                  _resident((d, IN_COLS), lambda i: (0, 0))],
        out_specs=pl.BlockSpec((tm, IN_COLS), lambda i: (i, 0)),
        out_shape=jax.ShapeDtypeStruct((n, IN_COLS), F32),
        compiler_params=_params(("parallel",)),
    )(xb, w)


MLA_Q_COLS = 3 * LANES


def _rms(x, g):
    return x * lax.rsqrt(jnp.mean(x * x, axis=-1, keepdims=True) + RMS_EPS) * g


def _mla_proj_kernel(cq_ref, kra_ref, krb_ref, ckv_ref, cos_ref, sin_ref, gq_ref, gkv_ref,
                     wuq_ref, wukv_ref, q_ref, k_ref, v_ref):
    cos = cos_ref[...]
    sin = sin_ref[...]
    scale = (MLA_NOPE + MLA_ROPE) ** -0.5 * math.log2(math.e)
    nq = _rms(cq_ref[...], gq_ref[...]).astype(BF16)
    nkv = _rms(ckv_ref[...], gkv_ref[...]).astype(BF16)
    k_rope = (kra_ref[...] * cos + krb_ref[...] * sin).astype(BF16)
    q_all = jnp.dot(nq, wuq_ref[...], preferred_element_type=F32)
    kv_all = jnp.dot(nkv, wukv_ref[...], preferred_element_type=F32)
    for h in range(MLA_HEADS):
        qh = q_all[:, h * MLA_Q_COLS:(h + 1) * MLA_Q_COLS]
        q_nope = qh[:, :LANES] * scale
        q_rope = (qh[:, LANES:2 * LANES] * cos + qh[:, 2 * LANES:] * sin) * scale
        q_ref[:, h * MLA_QK_PAD:h * MLA_QK_PAD + LANES] = q_nope.astype(BF16)
        q_ref[:, h * MLA_QK_PAD + LANES:(h + 1) * MLA_QK_PAD] = q_rope.astype(BF16)
        kv = kv_all[:, h * 2 * LANES:(h + 1) * 2 * LANES]
        k_ref[:, h * MLA_QK_PAD:h * MLA_QK_PAD + LANES] = kv[:, :LANES].astype(BF16)
        k_ref[:, h * MLA_QK_PAD + LANES:(h + 1) * MLA_QK_PAD] = k_rope
        v_ref[:, h * MLA_V:(h + 1) * MLA_V] = kv[:, LANES:].astype(BF16)


def mla_proj_call(z, cos_t, sin_t, gq, gkv, wuq, wukv, seq, tm):
    n = z.shape[0]
    nseq = seq // tm
    row = lambda i: (i, 0)
    const = lambda i: (0, 0)
    return pl.pallas_call(
        _mla_proj_kernel,
        grid=(n // tm,),
        in_specs=[pl.BlockSpec((tm, MLA_Q_RANK), lambda i: (i, COL_CQ // MLA_Q_RANK)),
                  pl.BlockSpec((tm, LANES), lambda i: (i, COL_KRA // LANES)),
                  pl.BlockSpec((tm, LANES), lambda i: (i, COL_KRB // LANES)),
                  pl.BlockSpec((tm, MLA_KV_RANK), lambda i: (i, COL_CKV // MLA_KV_RANK)),
                  pl.BlockSpec((tm, LANES), lambda i: (i % nseq, 0)),
                  pl.BlockSpec((tm, LANES), lambda i: (i % nseq, 0)),
                  pl.BlockSpec((1, MLA_Q_RANK), const),
                  pl.BlockSpec((1, MLA_KV_RANK), const),
                  _resident((MLA_Q_RANK, MLA_HEADS * MLA_Q_COLS), const),
                  _resident((MLA_KV_RANK, MLA_HEADS * 2 * LANES), const)],
        out_specs=[pl.BlockSpec((tm, MLA_HEADS * MLA_QK_PAD), row),
                   pl.BlockSpec((tm, MLA_HEADS * MLA_QK_PAD), row),
                   pl.BlockSpec((tm, MLA_WIDTH), row)],
        out_shape=[jax.ShapeDtypeStruct((n, MLA_HEADS * MLA_QK_PAD), BF16),
                   jax.ShapeDtypeStruct((n, MLA_HEADS * MLA_QK_PAD), BF16),
                   jax.ShapeDtypeStruct((n, MLA_WIDTH), BF16)],
        compiler_params=_params(("parallel",)),
    )(z, z, z, z, cos_t, sin_t, gq.reshape(1, -1), gkv.reshape(1, -1), wuq, wukv)


def _flash_kernel(q_ref, k_ref, v_ref, o_ref, m_sc, l_sc, acc_sc):
    j = pl.program_id(3)

    @pl.when(j == 0)
    def _():
        m_sc[...] = jnp.full(m_sc.shape, -jnp.inf, F32)
        l_sc[...] = jnp.zeros(l_sc.shape, F32)
        acc_sc[...] = jnp.zeros(acc_sc.shape, F32)

    s = lax.dot_general(q_ref[...], k_ref[...], (((1,), (1,)), ((), ())), preferred_element_type=F32)
    m_prev = m_sc[...]
    m_new = jnp.maximum(m_prev, jnp.max(s, axis=-1, keepdims=True))
    alpha = jnp.exp2(m_prev - m_new)
    p = jnp.exp2(s - m_new)
    l_sc[...] = alpha * l_sc[...] + jnp.sum(p, axis=-1, keepdims=True)
    acc_sc[...] = alpha * acc_sc[...] + jnp.dot(p.astype(BF16), v_ref[...], preferred_element_type=F32)
    m_sc[...] = m_new

    @pl.when(j == pl.num_programs(3) - 1)
    def _():
        o_ref[...] = (acc_sc[...] / l_sc[...]).astype(o_ref.dtype)


def flash_call(q, k, v, batch, seq, tq, tk):
    n = q.shape[0]
    nq, nk = seq // tq, seq // tk
    return pl.pallas_call(
        _flash_kernel,
        grid=(batch, MLA_HEADS, nq, nk),
        in_specs=[pl.BlockSpec((tq, MLA_QK_PAD), lambda b, h, i, j: (b * nq + i, h)),
                  pl.BlockSpec((tk, MLA_QK_PAD), lambda b, h, i, j: (b * nk + j, h)),
                  pl.BlockSpec((tk, MLA_V), lambda b, h, i, j: (b * nk + j, h))],
        out_specs=pl.BlockSpec((tq, MLA_V), lambda b, h, i, j: (b * nq + i, h)),
        out_shape=jax.ShapeDtypeStruct((n, MLA_WIDTH), BF16),
        scratch_shapes=[pltpu.VMEM((tq, 1), F32), pltpu.VMEM((tq, 1), F32), pltpu.VMEM((tq, MLA_V), F32)],
        compiler_params=_params(("parallel", "parallel", "parallel", "arbitrary")),
    )(q, k, v)


def _gla_block(q, k, v, a, wg, bg, states, reverse):
    t = q.shape[0]
    c = GLA_CHUNK
    gk = GLA_HEADS * GLA_DK
    heads = range(GLA_HEADS)
    ks = [slice(h * GLA_DK, (h + 1) * GLA_DK) for h in heads]
    vs = [slice(h * GLA_DV, (h + 1) * GLA_DV) for h in heads]
    nt = (((1,), (1,)), ((), ()))
    starts = list(range(0, t, c))
    pre = jnp.dot(a, wg, preferred_element_type=F32, precision=lax.Precision.HIGHEST) + bg
    log_a = jax.nn.log_sigmoid(pre) * (1.0 / GLA_NORMALIZER)
    row = lax.broadcasted_iota(I32, (t, t), 0)
    col = lax.broadcasted_iota(I32, (t, t), 1)
    same = lax.shift_right_logical(row, GLA_CHUNK_SHIFT) == lax.shift_right_logical(col, GLA_CHUNK_SHIFT)
    if reverse:
        tri = same & (col >= row)
        keep = same & (col > row)
    else:
        tri = same & (col <= row)
        keep = same & (col <= row)
    la0 = log_a.astype(BF16)
    r1 = log_a - la0.astype(F32)
    la1 = r1.astype(BF16)
    la2 = (r1 - la1.astype(F32)).astype(BF16)
    b3 = jnp.dot(jnp.where(tri, 1.0, 0.0).astype(BF16), jnp.concatenate([la0, la1, la2], axis=1),
                 preferred_element_type=F32)
    b = b3[:, :gk] + b3[:, gk:2 * gk] + b3[:, 2 * gk:]
    ends = [b[s0:s0 + 1, :] if reverse else b[s0 + c - 1:s0 + c, :] for s0 in starts]
    b_end = jnp.concatenate([jnp.broadcast_to(e, (c, gk)) for e in ends], axis=0)
    q_in = (q * (GLA_DK ** -0.5) * jnp.exp(b)).astype(BF16)
    k_in = (k * jnp.exp(-b)).astype(BF16)
    k_end = (k * jnp.exp(b_end - b)).astype(BF16)
    decay = [jnp.exp(e) for e in ends]
    vb = v.astype(BF16)
    att = [lax.dot_general(q_in[:, ks[h]], k_in[:, ks[h]], nt, preferred_element_type=F32) for h in heads]
    att = [jnp.where(keep, x, 0.0).astype(BF16) for x in att]
    o_intra = [jnp.dot(att[h], vb[:, vs[h]], preferred_element_type=F32) for h in heads]
    upd = [[lax.dot_general(vb[s0:s0 + c, vs[h]], k_end[s0:s0 + c, ks[h]], (((0,), (0,)), ((), ())),
                            preferred_element_type=F32) for h in heads] for s0 in starts]
    outs = [[None] * len(starts) for _ in heads]
    states = list(states)
    order = reversed(range(len(starts))) if reverse else range(len(starts))
    for ci in order:
        sl = slice(starts[ci], starts[ci] + c)
        for h in heads:
            outs[h][ci] = o_intra[h][sl, :] + lax.dot_general(q_in[sl, ks[h]], states[h].astype(BF16), nt,
                                                              preferred_element_type=F32)
            states[h] = states[h] * decay[ci][:, ks[h]] + upd[ci][h]
    return [jnp.concatenate(o, axis=0) for o in outs], states


def _gla_heads(q_ref, k_ref, v_ref, a_ref, wg_ref, bg_ref, st_sc, reverse):
    @pl.when(pl.program_id(1) == 0)
    def _():
        st_sc[...] = jnp.zeros(st_sc.shape, F32)

    outs, states = _gla_block(q_ref[...], k_ref[...], v_ref[...], a_ref[...], wg_ref[...], bg_ref[...],
                              [st_sc[h] for h in range(GLA_HEADS)], reverse)
    for h, st in enumerate(states):
        st_sc[h] = st
    return outs


def _gla_fwd_kernel(q_ref, k_ref, v_ref, a_ref, wg_ref, bg_ref, o_ref, st_sc):
    outs = _gla_heads(q_ref, k_ref, v_ref, a_ref, wg_ref, bg_ref, st_sc, False)
    for h, o in enumerate(outs):
        o_ref[:, h * GLA_DV:(h + 1) * GLA_DV] = o


def _gla_bwd_kernel(q_ref, k_ref, v_ref, a_ref, wg_ref, bg_ref, of_ref, r_ref, ng_ref, o_ref, st_sc):
    outs = _gla_heads(q_ref, k_ref, v_ref, a_ref, wg_ref, bg_ref, st_sc, True)
    ng = ng_ref[...]
    for h, o in enumerate(outs):
        vv = slice(h * GLA_DV, (h + 1) * GLA_DV)
        o = _rms(o + of_ref[:, vv], ng)
        r = r_ref[:, vv]
        o_ref[:, vv] = (o * (r * jax.nn.sigmoid(r))).astype(o_ref.dtype)


def gla_call(z, wg2, bg2, norm_g, batch, seq, t):
    n = z.shape[0]
    nb = seq // t
    gk = GLA_HEADS * GLA_DK
    fwd = lambda col: (lambda b, i: (b * nb + i, col))
    rev = lambda col: (lambda b, i: (b * nb + nb - 1 - i, col))
    common = dict(grid=(batch, nb),
                  scratch_shapes=[pltpu.VMEM((GLA_HEADS, GLA_DV, GLA_DK), F32)],
                  compiler_params=_params(("parallel", "arbitrary")))

    def z_specs(at, direction):
        return [pl.BlockSpec((t, gk), at(COL_Q // gk)),
                pl.BlockSpec((t, gk), at(COL_K // gk)),
                pl.BlockSpec((t, GLA_WIDTH), at(COL_V // GLA_WIDTH)),
                pl.BlockSpec((t, LANES), at(COL_KRA // LANES)),
                pl.BlockSpec((LANES, gk), lambda b, i: (0, direction)),
                pl.BlockSpec((1, gk), lambda b, i: (0, direction))]

    o_f = pl.pallas_call(
        _gla_fwd_kernel,
        in_specs=z_specs(fwd, 0),
        out_specs=pl.BlockSpec((t, GLA_WIDTH), fwd(0)),
        out_shape=jax.ShapeDtypeStruct((n, GLA_WIDTH), F32),
        **common,
    )(z, z, z, z, wg2, bg2)
    return pl.pallas_call(
        _gla_bwd_kernel,
        in_specs=z_specs(rev, 1) + [pl.BlockSpec((t, GLA_WIDTH), rev(0)),
                                    pl.BlockSpec((t, GLA_WIDTH), rev(COL_R // GLA_WIDTH)),
                                    pl.BlockSpec((1, GLA_DV), lambda b, i: (0, 0))],
        out_specs=pl.BlockSpec((t, GLA_WIDTH), rev(0)),
        out_shape=jax.ShapeDtypeStruct((n, GLA_WIDTH), BF16),
        **common,
    )(z, z, z, z, wg2, bg2, o_f, z, norm_g.reshape(1, GLA_DV))


def _outproj_kernel(alpha, a_ref, b_ref, x_ref, woa_ref, wob_ref, g_ref, bb_ref, wr_ref, br_ref,
                    h_ref, hb_ref, aff_ref):
    mix = jnp.dot(a_ref[...], woa_ref[...], preferred_element_type=F32)
    mix = mix + jnp.dot(b_ref[...], wob_ref[...], preferred_element_type=F32)
    h = _layer_norm(alpha * x_ref[...] + mix, g_ref[...], bb_ref[...])
    h_ref[...] = h
    hb = h.astype(BF16)
    hb_ref[...] = hb
    h_lo = (h - hb.astype(F32)).astype(BF16)
    l1 = jnp.dot(hb, wr_ref[...], preferred_element_type=F32)
    l2 = jnp.dot(h_lo, wr_ref[:, :LANES], preferred_element_type=F32)
    logits = (l1[:, :LANES] + l1[:, LANES:] + l2)[:, :N_EXPERTS] + br_ref[...]
    e = jnp.exp(logits - jnp.max(logits, axis=-1, keepdims=True))
    aff_ref[...] = e / jnp.sum(e, axis=-1, keepdims=True)


def outproj_call(mix_a, mix_b, x, wo_a, wo_b, g, b, wr, br, alpha, tm):
    n, d = x.shape
    row = lambda i: (i, 0)
    const = lambda i: (0, 0)
    return pl.pallas_call(
        functools.partial(_outproj_kernel, alpha),
        grid=(n // tm,),
        in_specs=[pl.BlockSpec((tm, GLA_WIDTH), row),
                  pl.BlockSpec((tm, MLA_WIDTH), row),
                  pl.BlockSpec((tm, d), row),
                  _resident((GLA_WIDTH, d), const),
                  _resident((MLA_WIDTH, d), const),
                  pl.BlockSpec((1, d), const),
                  pl.BlockSpec((1, d), const),
                  _resident((d, 2 * LANES), const),
                  pl.BlockSpec((1, N_EXPERTS), const)],
        out_specs=[pl.BlockSpec((tm, d), row), pl.BlockSpec((tm, d), row),
                   pl.BlockSpec((tm, N_EXPERTS), row)],
        out_shape=[jax.ShapeDtypeStruct((n, d), F32), jax.ShapeDtypeStruct((n, d), BF16),
                   jax.ShapeDtypeStruct((n, N_EXPERTS), F32)],
        compiler_params=_params(("parallel",)),
    )(mix_a, mix_b, x, wo_a, wo_b, g.reshape(1, d), b.reshape(1, d), wr, br.reshape(1, N_EXPERTS))


def _threshold_kernel(cap, a_ref, thr_ref, need_ref):
    bits = pltpu.bitcast(a_ref[...], I32)

    def count(pred):
        part = jnp.sum(pred.astype(I32), axis=1, keepdims=True)
        return jnp.sum(part, axis=2, keepdims=True)

    def body(i, thr):
        cand = thr | jnp.left_shift(jnp.int32(1), 30 - i)
        return jnp.where(count(bits >= cand) >= cap, cand, thr)

    thr = lax.fori_loop(0, 31, body, jnp.zeros((N_EXPERTS, 1, 1), I32))
    thr_ref[...] = jnp.broadcast_to(thr, thr_ref.shape)
    need_ref[...] = jnp.broadcast_to(cap - count(bits > thr), need_ref.shape)


def threshold_call(aff_t, cap):
    e, n = aff_t.shape
    full = lambda: (0, 0, 0)
    return pl.pallas_call(
        functools.partial(_threshold_kernel, cap),
        in_specs=[pl.BlockSpec((e, n // LANES, LANES), full)],
        out_specs=[pl.BlockSpec((e, 1, LANES), full), pl.BlockSpec((e, 1, LANES), full)],
        out_shape=[jax.ShapeDtypeStruct((e, 1, LANES), I32), jax.ShapeDtypeStruct((e, 1, LANES), I32)],
        compiler_params=_params(None),
    )(aff_t.reshape(e, n // LANES, LANES))


LIST_TID_HI, LIST_TID_LO, LIST_G0, LIST_G1, LIST_G2 = 0, 1, 2, 3, 4
TID_SHIFT = 8
TID_SPLIT = 1 << TID_SHIFT


LIST_GROUP = 8


def _list_kernel(thr_ref, need_ref, a_ref, u_ref, o_ref):
    e0 = pl.program_id(0) * LIST_GROUP
    _, nt, t = a_ref.shape
    o_ref[...] = jnp.zeros(o_ref.shape, F32)
    u = u_ref[...]
    sub8 = lax.broadcasted_iota(I32, (8, t), 0)
    subv = lax.broadcasted_iota(I32, (LANES, t), 0)
    slot = lax.broadcasted_iota(I32, (t, t), 0)
    lane = lax.broadcasted_iota(I32, (1, t), 1)

    def one_expert(g, j, gt_before, eq_before):
        thr = thr_ref[e0 + g]
        need = need_ref[e0 + g]
        a = a_ref[g, pl.ds(j, 1), :]
        bits = pltpu.bitcast(a, I32)
        gt = bits > thr
        eq = bits == thr
        lhs = jnp.where(sub8 == 0, jnp.where(gt, 1.0, 0.0), jnp.where(sub8 == 1, jnp.where(eq, 1.0, 0.0), 0.0))
        cs = jnp.dot(lhs.astype(BF16), u, preferred_element_type=F32).astype(I32)
        c_gt = cs[0:1, :]
        c_eq = cs[1:2, :]
        tie_rank = eq_before + c_eq
        sel = gt | (eq & (tie_rank <= need))
        off = gt_before + jnp.minimum(eq_before, need)
        dest = gt_before + c_gt + jnp.minimum(tie_rank, need) - 1 - off
        dest = jnp.where(sel, dest, -1)
        onehot = (slot == dest).astype(BF16)

        tid = j * t + lane
        g0 = a.astype(BF16).astype(F32)
        g1 = (a - g0).astype(BF16).astype(F32)
        g2 = (a - g0 - g1).astype(BF16).astype(F32)
        vals = jnp.where(subv == LIST_TID_HI, lax.shift_right_logical(tid, TID_SHIFT).astype(F32),
               jnp.where(subv == LIST_TID_LO, (tid & (TID_SPLIT - 1)).astype(F32),
               jnp.where(subv == LIST_G0, g0,
               jnp.where(subv == LIST_G1, g1,
               jnp.where(subv == LIST_G2, g2, 0.0)))))
        rows = lax.dot_general(onehot, vals.astype(BF16), (((1,), (1,)), ((), ())), preferred_element_type=F32)
        win = pl.ds(off, t)
        o_ref[g, win, :] = o_ref[g, win, :] + rows
        return gt_before + jnp.sum(gt.astype(I32)), eq_before + jnp.sum(eq.astype(I32))

    def tile(j, carry):
        out = []
        for g in range(LIST_GROUP):
            out.extend(one_expert(g, j, carry[2 * g], carry[2 * g + 1]))
        return tuple(out)

    lax.fori_loop(0, nt, tile, (jnp.int32(0),) * (2 * LIST_GROUP))


def list_call(thr, need, aff_t, cap, t):
    e, n = aff_t.shape
    cpad = cap + t
    u = (jnp.arange(t)[:, None] <= jnp.arange(t)[None, :]).astype(BF16)
    return pl.pallas_call(
        _list_kernel,
        grid_spec=pltpu.PrefetchScalarGridSpec(
            num_scalar_prefetch=2,
            grid=(e // LIST_GROUP,),
            in_specs=[pl.BlockSpec((LIST_GROUP, n // t, t), lambda ei, *_: (ei, 0, 0)),
                      pl.BlockSpec((t, t), lambda ei, *_: (0, 0))],
            out_specs=pl.BlockSpec((LIST_GROUP, cpad, LANES), lambda ei, *_: (ei, 0, 0))),
        out_shape=jax.ShapeDtypeStruct((e, cpad, LANES), F32),
        compiler_params=_params(("arbitrary",)),
    )(thr, need, aff_t.reshape(e, n // t, t), u)


def _ple_kernel(alpha, p_ref, hb_ref, h_ref, wp_ref, wg_ref, o_ref):
    ple = jnp.dot(p_ref[...].astype(BF16), wp_ref[...], preferred_element_type=F32)
    gate = jnp.dot(hb_ref[...], wg_ref[...], preferred_element_type=F32)
    o_ref[...] = alpha * h_ref[...] + ple * jax.nn.sigmoid(gate)


def ple_call(p, hb, h, wp, wg, alpha, tm, layer):
    n, d = h.shape
    row = lambda i: (i, 0)
    const = lambda i: (0, 0)
    return pl.pallas_call(
        functools.partial(_ple_kernel, alpha),
        grid=(n // tm,),
        in_specs=[pl.BlockSpec((None, tm, PLE_DIM), lambda i: (layer, i, 0)),
                  pl.BlockSpec((tm, d), row), pl.BlockSpec((tm, d), row),
                  _resident((PLE_DIM, d), const), _resident((d, d), const)],
        out_specs=pl.BlockSpec((tm, d), row),
        out_shape=jax.ShapeDtypeStruct((n, d), F32),
        compiler_params=_params(("parallel",)),
    )(p, hb, h, wp, wg)


def _moe_kernel(idx_ref, lst_ref, wg_ref, wu_ref, wd_ref, h_hbm, acc_in, acc_hbm,
                xbuf, xb_sc, abuf, xsem, asem, ssem):
    del acc_in
    tm = abuf.shape[0]
    nk = pl.num_programs(1)
    step = pl.program_id(0) * nk + pl.program_id(1)
    last = pl.num_programs(0) * nk - 1
    slot = step % 2

    def gather_x(s, sl):
        for r in range(tm):
            tok = idx_ref[s * tm + r]
            pltpu.make_async_copy(h_hbm.at[pl.ds(tok, 1), :], xbuf.at[sl, pl.ds(r, 1), :], xsem.at[sl]).start()

    def gather_acc(s):
        for r in range(tm):
            tok = idx_ref[s * tm + r]
            pltpu.make_async_copy(acc_hbm.at[pl.ds(tok, 1), :], abuf.at[pl.ds(r, 1), :], asem).start()

    def scatter_acc(s):
        for r in range(tm):
            tok = idx_ref[s * tm + r]
            pltpu.make_async_copy(abuf.at[pl.ds(r, 1), :], acc_hbm.at[pl.ds(tok, 1), :], ssem).start()

    def rows_done(buf, sem):
        pltpu.make_async_copy(h_hbm.at[pl.ds(0, tm), :], buf, sem).wait()

    @pl.when(step == 0)
    def _():
        gather_x(step, slot)
        gather_acc(step)
        rows_done(abuf, asem)

    rows_done(xbuf.at[slot], xsem.at[slot])
    xb_sc[...] = xbuf[slot].astype(BF16)
    gather_x(jnp.minimum(step + 1, last), 1 - slot)
    scatter_acc(jnp.maximum(step - 1, 0))
    xb = xb_sc[...]
    g = jnp.dot(xb, wg_ref[...], preferred_element_type=F32)
    u = jnp.dot(xb, wu_ref[...], preferred_element_type=F32)
    hid = (g * jax.nn.sigmoid(g) * u).astype(BF16)

    rows_done(abuf, ssem)
    gather_acc(step)
    y = jnp.dot(hid, wd_ref[...], preferred_element_type=F32)
    lst = lst_ref[...]
    gate = lst[:, LIST_G0:LIST_G0 + 1] + lst[:, LIST_G1:LIST_G1 + 1] + lst[:, LIST_G2:LIST_G2 + 1]
    y = y * gate
    rows_done(abuf, asem)
    abuf[...] = abuf[...] + y

    @pl.when(step == last)
    def _():
        scatter_acc(step)
        rows_done(abuf, ssem)
        rows_done(xbuf.at[1 - slot], xsem.at[1 - slot])


def moe_call(idx, lists, wg, wu, wd, h, acc, cap, tm, layer):
    n, d = h.shape
    e = wg.shape[1]
    ff = wg.shape[3]
    expert = lambda ei, k, *_: (layer, ei, 0, 0)
    return pl.pallas_call(
        _moe_kernel,
        grid_spec=pltpu.PrefetchScalarGridSpec(
            num_scalar_prefetch=1,
            grid=(e, cap // tm),
            in_specs=[pl.BlockSpec((None, tm, LANES), lambda ei, k, *_: (ei, k, 0)),
                      _resident((None, None, d, ff), expert),
                      _resident((None, None, d, ff), expert),
                      _resident((None, None, ff, d), expert),
                      pl.BlockSpec(memory_space=pl.ANY),
                      pl.BlockSpec(memory_space=pl.ANY)],
            out_specs=pl.BlockSpec(memory_space=pl.ANY),
            scratch_shapes=[pltpu.VMEM((2, tm, d), F32), pltpu.VMEM((tm, d), BF16), pltpu.VMEM((tm, d), F32),
                            pltpu.SemaphoreType.DMA((2,)), pltpu.SemaphoreType.DMA(()),
                            pltpu.SemaphoreType.DMA(())]),
        out_shape=jax.ShapeDtypeStruct((n, d), F32),
        input_output_aliases={6: 0},
        compiler_params=_params(("arbitrary", "arbitrary"), has_side_effects=True, disable_bounds_checks=True),
    )(idx, lists, wg, wu, wd, h, acc)


def _rotate_half_cols(w):
    half = MLA_ROPE // 2
    return jnp.concatenate([-w[..., half:], w[..., :half]], axis=-1)


def _prep_weights(w_in, gla_wg2_f, gla_bg_f, gla_wg2_b, gla_bg_b, mla_w_uq, w_o, w_gate_e, w_up_e,
                  w_down_e, w_ple, w_ple_gate):
    depth = w_in.shape[0]
    gk = GLA_HEADS * GLA_DK
    o = 0
    parts = {}
    for name, width in (("q", gk), ("k", gk), ("v", GLA_WIDTH), ("r", GLA_WIDTH), ("gf", GLA_GATE_RANK),
                        ("gb", GLA_GATE_RANK), ("cq", MLA_Q_RANK), ("ckv", MLA_KV_RANK), ("kr", MLA_ROPE)):
        parts[name] = w_in[:, :, o:o + width]
        o += width
    zeros = lambda w: jnp.zeros((depth, D_MODEL, w), w_in.dtype)
    kra = jnp.concatenate([parts["kr"], parts["gf"], parts["gb"],
                           zeros(LANES - MLA_ROPE - 2 * GLA_GATE_RANK)], axis=-1)
    krb = jnp.concatenate([_rotate_half_cols(parts["kr"]), zeros(LANES - MLA_ROPE)], axis=-1)
    w_in2 = jnp.concatenate([parts["q"], parts["k"], parts["v"], parts["r"], parts["cq"], kra, krb,
                             parts["ckv"]], axis=-1).astype(BF16)

    uq = mla_w_uq.reshape(depth, MLA_Q_RANK, MLA_HEADS, MLA_NOPE + MLA_ROPE)
    rope = uq[..., MLA_NOPE:]
    zpad = jnp.zeros(rope.shape[:-1] + (LANES - MLA_ROPE,), rope.dtype)
    wuq2 = jnp.concatenate([uq[..., :MLA_NOPE], rope, zpad, _rotate_half_cols(rope), zpad], axis=-1)
    wuq2 = wuq2.reshape(depth, MLA_Q_RANK, MLA_HEADS * MLA_Q_COLS).astype(BF16)

    wg2 = jnp.zeros((depth, LANES, 2 * gk), F32)
    wg2 = wg2.at[:, GATE_ROW_F:GATE_ROW_F + GLA_GATE_RANK, :gk].set(gla_wg2_f)
    wg2 = wg2.at[:, GATE_ROW_B:GATE_ROW_B + GLA_GATE_RANK, gk:].set(gla_wg2_b)
    bg2 = jnp.concatenate([gla_bg_f, gla_bg_b], axis=-1).reshape(depth, 1, 2 * gk)
    per_layer = lambda w: [w[i].astype(BF16) for i in range(depth)]
    return dict(w_in=w_in2, wuq=wuq2, wg2=wg2, bg2=bg2,
                wo_a=per_layer(w_o[:, :GLA_WIDTH]), wo_b=per_layer(w_o[:, GLA_WIDTH:]),
                wge=w_gate_e.astype(BF16), wue=w_up_e.astype(BF16), wde=w_down_e.astype(BF16),
                wple=per_layer(w_ple), wpg=per_layer(w_ple_gate))


def _rope_tables(seq):
    pos = jnp.arange(seq, dtype=F32)
    inv = ROPE_THETA ** (-jnp.arange(0, MLA_ROPE, 2, dtype=F32) / MLA_ROPE)
    ang = pos[:, None] * inv[None, :]
    zpad = jnp.zeros((seq, LANES - MLA_ROPE), F32)
    cos_t = jnp.concatenate([jnp.cos(ang), jnp.cos(ang), zpad], axis=-1)
    sin_t = jnp.concatenate([jnp.sin(ang), jnp.sin(ang), zpad], axis=-1)
    return cos_t, sin_t


def _tiles(batch, seq):
    n = batch * seq
    return dict(row=min(512, seq), gla=min(256, seq), tq=min(1024, seq),
                tk=min(8192, seq), lst=min(256, n), moe=min(256, CAPACITY_FACTOR * n // N_EXPERTS))


def _trunk(x, p, ln_in_g, ln_in_b, pw, gla_norm_g, mla_qnorm_g, mla_kvnorm_g, mla_w_ukv, ln1_g, ln1_b,
           w_router, b_router, ln2_g, ln2_b):
    batch, seq, d = x.shape
    depth = p.shape[0]
    n = batch * seq
    cap = CAPACITY_FACTOR * n // N_EXPERTS
    alpha = (2 * depth) ** 0.25
    tl = _tiles(batch, seq)
    cos_t, sin_t = _rope_tables(seq)
    wukv = mla_w_ukv.astype(BF16)
    wr_hi = w_router.astype(BF16)
    wr_lo = (w_router - wr_hi.astype(F32)).astype(BF16)
    wr_pad = jnp.zeros((depth, d, LANES - N_EXPERTS), BF16)
    wr2 = jnp.concatenate([wr_hi, wr_pad, wr_lo, wr_pad], axis=-1)
    xf, xb = layer_norm_call(x.reshape(n, d), ln_in_g, ln_in_b, tl["row"])
    for i in range(depth):
        z = inproj_call(xb, pw["w_in"][i], tl["row"])
        mix_a = gla_call(z, pw["wg2"][i], pw["bg2"][i], gla_norm_g[i], batch, seq, tl["gla"])
        q, k, v = mla_proj_call(z, cos_t, sin_t, mla_qnorm_g[i], mla_kvnorm_g[i], pw["wuq"][i], wukv[i],
                                seq, tl["row"])
        mix_b = flash_call(q, k, v, batch, seq, tl["tq"], tl["tk"])
        h, hb, aff = outproj_call(mix_a, mix_b, xf, pw["wo_a"][i], pw["wo_b"][i], ln1_g[i], ln1_b[i],
                                  wr2[i], b_router[i], alpha, tl["row"])
        aff_t = aff.T
        thr, need = threshold_call(aff_t, cap)
        lists = list_call(thr[:, 0, 0], need[:, 0, 0], aff_t, cap, tl["lst"])
        idx = (lists[:, :cap, LIST_TID_HI] * TID_SPLIT + lists[:, :cap, LIST_TID_LO]).astype(I32).reshape(-1)
        acc = ple_call(p.reshape(depth, n, PLE_DIM), hb, h, pw["wple"][i], pw["wpg"][i], alpha, tl["row"], i)
        acc = moe_call(idx, lists, pw["wge"], pw["wue"], pw["wde"], h, acc, cap, tl["moe"], i)
        xf, xb = layer_norm_call(acc, ln2_g[i], ln2_b[i], tl["row"])
    return xf.reshape(batch, seq, d)


def kernel(x_prompt, x_sample, p_prompt, p_sample, ln_in_g, ln_in_b, w_in, gla_wg2_f, gla_bg_f, gla_wg2_b, gla_bg_b, gla_norm_g, mla_qnorm_g, mla_kvnorm_g, mla_w_uq, mla_w_ukv, w_o, ln1_g, ln1_b, w_router, b_router, w_gate_e, w_up_e, w_down_e, w_ple, w_ple_gate, ln2_g, ln2_b):
    pw = _prep_weights(w_in, gla_wg2_f, gla_bg_f, gla_wg2_b, gla_bg_b, mla_w_uq, w_o, w_gate_e, w_up_e,
                       w_down_e, w_ple, w_ple_gate)
    rest = (ln_in_g, ln_in_b, pw, gla_norm_g, mla_qnorm_g, mla_kvnorm_g, mla_w_ukv, ln1_g, ln1_b,
            w_router, b_router, ln2_g, ln2_b)
    return (_trunk(x_prompt, p_prompt, *rest), _trunk(x_sample, p_sample, *rest))
```

```python
import functools
import math

import jax
import jax.numpy as jnp
from jax import lax
from jax.experimental import pallas as pl
from jax.experimental.pallas import tpu as pltpu

F32 = jnp.float32
BF16 = jnp.bfloat16
I32 = jnp.int32

D_MODEL = 2048
GLA_HEADS = 4
GLA_DK = 128
GLA_DV = 256
GLA_WIDTH = GLA_HEADS * GLA_DV
GLA_GATE_RANK = 16
GLA_NORMALIZER = 16.0
GLA_CHUNK = 64
GLA_CHUNK_SHIFT = 6
MLA_HEADS = 8
MLA_NOPE = 128
MLA_ROPE = 64
MLA_V = 128
MLA_WIDTH = MLA_HEADS * MLA_V
MLA_Q_RANK = 768
MLA_KV_RANK = 512
ROPE_THETA = 10000.0
N_EXPERTS = 16
CAPACITY_FACTOR = 2
PLE_DIM = 256
LN_EPS = 1e-5
RMS_EPS = 1e-6

LANES = 128
MLA_QK_PAD = 256
VMEM_LIMIT = 56 * 1024 * 1024

COL_Q = 0
COL_K = 512
COL_V = 1024
COL_R = 2048
COL_CQ = 3072
COL_KRA = 3840
COL_KRB = 3968
COL_CKV = 4096
IN_COLS = 4608
GATE_ROW_F = MLA_ROPE
GATE_ROW_B = MLA_ROPE + GLA_GATE_RANK


def _params(sem, vmem=VMEM_LIMIT, **kw):
    return pltpu.CompilerParams(dimension_semantics=sem, vmem_limit_bytes=vmem, **kw)


def _resident(shape, index_map):
    return pl.BlockSpec(shape, index_map, pipeline_mode=pl.Buffered(1))


def _layer_norm(y, g, b):
    mu = jnp.mean(y, axis=-1, keepdims=True)
    yc = y - mu
    var = jnp.mean(yc * yc, axis=-1, keepdims=True)
    return yc * lax.rsqrt(var + LN_EPS) * g + b


def _ln_kernel(x_ref, g_ref, b_ref, o_ref, ob_ref):
    y = _layer_norm(x_ref[...], g_ref[...], b_ref[...])
    o_ref[...] = y
    ob_ref[...] = y.astype(BF16)


def layer_norm_call(x, g, b, tm):
    n, d = x.shape
    return pl.pallas_call(
        _ln_kernel,
        grid=(n // tm,),
        in_specs=[pl.BlockSpec((tm, d), lambda i: (i, 0)),
                  pl.BlockSpec((1, d), lambda i: (0, 0)),
                  pl.BlockSpec((1, d), lambda i: (0, 0))],
        out_specs=[pl.BlockSpec((tm, d), lambda i: (i, 0)),
                   pl.BlockSpec((tm, d), lambda i: (i, 0))],
        out_shape=[jax.ShapeDtypeStruct((n, d), F32), jax.ShapeDtypeStruct((n, d), BF16)],
        compiler_params=_params(("parallel",)),
    )(x, g.reshape(1, d), b.reshape(1, d))


IN_CHUNK = 512


def _inproj_kernel(x_ref, w_ref, o_ref):
    x = x_ref[...]
    for c in range(0, IN_COLS, IN_CHUNK):
        o_ref[:, c:c + IN_CHUNK] = jnp.dot(x, w_ref[:, c:c + IN_CHUNK], preferred_element_type=F32)


def inproj_call(xb, w, tm):
    n, d = xb.shape
    return pl.pallas_call(
        _inproj_kernel,
        grid=(n // tm,),
        in_specs=[pl.BlockSpec((tm, d), lambda i: (i, 0)),
                  _resident((d, IN_COLS), lambda i: (0, 0))],
        out_specs=pl.BlockSpec((tm, IN_COLS), lambda i: (i, 0)),
        out_shape=jax.ShapeDtypeStruct((n, IN_COLS), F32),
        compiler_params=_params(("parallel",)),
    )(xb, w)


MLA_Q_COLS = 3 * LANES


def _rms(x, g):
    return x * lax.rsqrt(jnp.mean(x * x, axis=-1, keepdims=True) + RMS_EPS) * g


def _mla_proj_kernel(cq_ref, kra_ref, krb_ref, ckv_ref, cos_ref, sin_ref, gq_ref, gkv_ref,
                     wuq_ref, wukv_ref, q_ref, k_ref, v_ref):
    cos = cos_ref[...]
    sin = sin_ref[...]
    scale = (MLA_NOPE + MLA_ROPE) ** -0.5 * math.log2(math.e)
    nq = _rms(cq_ref[...], gq_ref[...]).astype(BF16)
    nkv = _rms(ckv_ref[...], gkv_ref[...]).astype(BF16)
    k_rope = (kra_ref[...] * cos + krb_ref[...] * sin).astype(BF16)
    q_all = jnp.dot(nq, wuq_ref[...], preferred_element_type=F32)
    kv_all = jnp.dot(nkv, wukv_ref[...], preferred_element_type=F32)
    for h in range(MLA_HEADS):
        qh = q_all[:, h * MLA_Q_COLS:(h + 1) * MLA_Q_COLS]
        q_nope = qh[:, :LANES] * scale
        q_rope = (qh[:, LANES:2 * LANES] * cos + qh[:, 2 * LANES:] * sin) * scale
        q_ref[:, h * MLA_QK_PAD:h * MLA_QK_PAD + LANES] = q_nope.astype(BF16)
        q_ref[:, h * MLA_QK_PAD + LANES:(h + 1) * MLA_QK_PAD] = q_rope.astype(BF16)
        kv = kv_all[:, h * 2 * LANES:(h + 1) * 2 * LANES]
        k_ref[:, h * MLA_QK_PAD:h * MLA_QK_PAD + LANES] = kv[:, :LANES].astype(BF16)
        k_ref[:, h * MLA_QK_PAD + LANES:(h + 1) * MLA_QK_PAD] = k_rope
        v_ref[:, h * MLA_V:(h + 1) * MLA_V] = kv[:, LANES:].astype(BF16)


def mla_proj_call(z, cos_t, sin_t, gq, gkv, wuq, wukv, seq, tm):
    n = z.shape[0]
    nseq = seq // tm
    row = lambda i: (i, 0)
    const = lambda i: (0, 0)
    return pl.pallas_call(
        _mla_proj_kernel,
        grid=(n // tm,),
        in_specs=[pl.BlockSpec((tm, MLA_Q_RANK), lambda i: (i, COL_CQ // MLA_Q_RANK)),
                  pl.BlockSpec((tm, LANES), lambda i: (i, COL_KRA // LANES)),
                  pl.BlockSpec((tm, LANES), lambda i: (i, COL_KRB // LANES)),
                  pl.BlockSpec((tm, MLA_KV_RANK), lambda i: (i, COL_CKV // MLA_KV_RANK)),
                  pl.BlockSpec((tm, LANES), lambda i: (i % nseq, 0)),
                  pl.BlockSpec((tm, LANES), lambda i: (i % nseq, 0)),
                  pl.BlockSpec((1, MLA_Q_RANK), const),
                  pl.BlockSpec((1, MLA_KV_RANK), const),
                  _resident((MLA_Q_RANK, MLA_HEADS * MLA_Q_COLS), const),
                  _resident((MLA_KV_RANK, MLA_HEADS * 2 * LANES), const)],
        out_specs=[pl.BlockSpec((tm, MLA_HEADS * MLA_QK_PAD), row),
                   pl.BlockSpec((tm, MLA_HEADS * MLA_QK_PAD), row),
                   pl.BlockSpec((tm, MLA_WIDTH), row)],
        out_shape=[jax.ShapeDtypeStruct((n, MLA_HEADS * MLA_QK_PAD), BF16),
                   jax.ShapeDtypeStruct((n, MLA_HEADS * MLA_QK_PAD), BF16),
                   jax.ShapeDtypeStruct((n, MLA_WIDTH), BF16)],
        compiler_params=_params(("parallel",)),
    )(z, z, z, z, cos_t, sin_t, gq.reshape(1, -1), gkv.reshape(1, -1), wuq, wukv)


def _flash_kernel(q_ref, k_ref, v_ref, o_ref, m_sc, l_sc, acc_sc):
    j = pl.program_id(3)

    @pl.when(j == 0)
    def _():
        m_sc[...] = jnp.full(m_sc.shape, -jnp.inf, F32)
        l_sc[...] = jnp.zeros(l_sc.shape, F32)
        acc_sc[...] = jnp.zeros(acc_sc.shape, F32)

    s = lax.dot_general(q_ref[...], k_ref[...], (((1,), (1,)), ((), ())), preferred_element_type=F32)
    m_prev = m_sc[...]
    m_new = jnp.maximum(m_prev, jnp.max(s, axis=-1, keepdims=True))
    alpha = jnp.exp2(m_prev - m_new)
    p = jnp.exp2(s - m_new)
    l_sc[...] = alpha * l_sc[...] + jnp.sum(p, axis=-1, keepdims=True)
    acc_sc[...] = alpha * acc_sc[...] + jnp.dot(p.astype(BF16), v_ref[...], preferred_element_type=F32)
    m_sc[...] = m_new

    @pl.when(j == pl.num_programs(3) - 1)
    def _():
        o_ref[...] = (acc_sc[...] / l_sc[...]).astype(o_ref.dtype)


def flash_call(q, k, v, batch, seq, tq, tk):
    n = q.shape[0]
    nq, nk = seq // tq, seq // tk
    return pl.pallas_call(
        _flash_kernel,
        grid=(batch, MLA_HEADS, nq, nk),
        in_specs=[pl.BlockSpec((tq, MLA_QK_PAD), lambda b, h, i, j: (b * nq + i, h)),
                  pl.BlockSpec((tk, MLA_QK_PAD), lambda b, h, i, j: (b * nk + j, h)),
                  pl.BlockSpec((tk, MLA_V), lambda b, h, i, j: (b * nk + j, h))],
        out_specs=pl.BlockSpec((tq, MLA_V), lambda b, h, i, j: (b * nq + i, h)),
        out_shape=jax.ShapeDtypeStruct((n, MLA_WIDTH), BF16),
        scratch_shapes=[pltpu.VMEM((tq, 1), F32), pltpu.VMEM((tq, 1), F32), pltpu.VMEM((tq, MLA_V), F32)],
        compiler_params=_params(("parallel", "parallel", "parallel", "arbitrary")),
    )(q, k, v)


def _gla_block(q, k, v, a, wg, bg, states, reverse):
    t = q.shape[0]
    c = GLA_CHUNK
    gk = GLA_HEADS * GLA_DK
    heads = range(GLA_HEADS)
    ks = [slice(h * GLA_DK, (h + 1) * GLA_DK) for h in heads]
    vs = [slice(h * GLA_DV, (h + 1) * GLA_DV) for h in heads]
    nt = (((1,), (1,)), ((), ()))
    starts = list(range(0, t, c))
    pre = jnp.dot(a, wg, preferred_element_type=F32, precision=lax.Precision.HIGHEST) + bg
    log_a = jax.nn.log_sigmoid(pre) * (1.0 / GLA_NORMALIZER)
    row = lax.broadcasted_iota(I32, (t, t), 0)
    col = lax.broadcasted_iota(I32, (t, t), 1)
    same = lax.shift_right_logical(row, GLA_CHUNK_SHIFT) == lax.shift_right_logical(col, GLA_CHUNK_SHIFT)
    if reverse:
        tri = same & (col >= row)
        keep = same & (col > row)
    else:
        tri = same & (col <= row)
        keep = same & (col <= row)
    la0 = log_a.astype(BF16)
    r1 = log_a - la0.astype(F32)
    la1 = r1.astype(BF16)
    la2 = (r1 - la1.astype(F32)).astype(BF16)
    b3 = jnp.dot(jnp.where(tri, 1.0, 0.0).astype(BF16), jnp.concatenate([la0, la1, la2], axis=1),
                 preferred_element_type=F32)
    b = b3[:, :gk] + b3[:, gk:2 * gk] + b3[:, 2 * gk:]
    ends = [b[s0:s0 + 1, :] if reverse else b[s0 + c - 1:s0 + c, :] for s0 in starts]
    b_end = jnp.concatenate([jnp.broadcast_to(e, (c, gk)) for e in ends], axis=0)
    q_in = (q * (GLA_DK ** -0.5) * jnp.exp(b)).astype(BF16)
    k_in = (k * jnp.exp(-b)).astype(BF16)
    k_end = (k * jnp.exp(b_end - b)).astype(BF16)
    decay = [jnp.exp(e) for e in ends]
    vb = v.astype(BF16)
    att = [lax.dot_general(q_in[:, ks[h]], k_in[:, ks[h]], nt, preferred_element_type=F32) for h in heads]
    att = [jnp.where(keep, x, 0.0).astype(BF16) for x in att]
    o_intra = [jnp.dot(att[h], vb[:, vs[h]], preferred_element_type=F32) for h in heads]
    upd = [[lax.dot_general(vb[s0:s0 + c, vs[h]], k_end[s0:s0 + c, ks[h]], (((0,), (0,)), ((), ())),
                            preferred_element_type=F32) for h in heads] for s0 in starts]
    outs = [[None] * len(starts) for _ in heads]
    states = list(states)
    order = reversed(range(len(starts))) if reverse else range(len(starts))
    for ci in order:
        sl = slice(starts[ci], starts[ci] + c)
        for h in heads:
            outs[h][ci] = o_intra[h][sl, :] + lax.dot_general(q_in[sl, ks[h]], states[h].astype(BF16), nt,
                                                              preferred_element_type=F32)
            states[h] = states[h] * decay[ci][:, ks[h]] + upd[ci][h]
    return [jnp.concatenate(o, axis=0) for o in outs], states


def _gla_heads(q_ref, k_ref, v_ref, a_ref, wg_ref, bg_ref, st_sc, reverse):
    @pl.when(pl.program_id(1) == 0)
    def _():
        st_sc[...] = jnp.zeros(st_sc.shape, F32)

    outs, states = _gla_block(q_ref[...], k_ref[...], v_ref[...], a_ref[...], wg_ref[...], bg_ref[...],
                              [st_sc[h] for h in range(GLA_HEADS)], reverse)
    for h, st in enumerate(states):
        st_sc[h] = st
    return outs


def _gla_fwd_kernel(q_ref, k_ref, v_ref, a_ref, wg_ref, bg_ref, o_ref, st_sc):
    outs = _gla_heads(q_ref, k_ref, v_ref, a_ref, wg_ref, bg_ref, st_sc, False)
    for h, o in enumerate(outs):
        o_ref[:, h * GLA_DV:(h + 1) * GLA_DV] = o


def _gla_bwd_kernel(q_ref, k_ref, v_ref, a_ref, wg_ref, bg_ref, of_ref, r_ref, ng_ref, o_ref, st_sc):
    outs = _gla_heads(q_ref, k_ref, v_ref, a_ref, wg_ref, bg_ref, st_sc, True)
    ng = ng_ref[...]
    for h, o in enumerate(outs):
        vv = slice(h * GLA_DV, (h + 1) * GLA_DV)
        o = _rms(o + of_ref[:, vv], ng)
        r = r_ref[:, vv]
        o_ref[:, vv] = (o * (r * jax.nn.sigmoid(r))).astype(o_ref.dtype)


def gla_call(z, wg2, bg2, norm_g, batch, seq, t):
    n = z.shape[0]
    nb = seq // t
    gk = GLA_HEADS * GLA_DK
    fwd = lambda col: (lambda b, i: (b * nb + i, col))
    rev = lambda col: (lambda b, i: (b * nb + nb - 1 - i, col))
    common = dict(grid=(batch, nb),
                  scratch_shapes=[pltpu.VMEM((GLA_HEADS, GLA_DV, GLA_DK), F32)],
                  compiler_params=_params(("parallel", "arbitrary")))

    def z_specs(at, direction):
        return [pl.BlockSpec((t, gk), at(COL_Q // gk)),
                pl.BlockSpec((t, gk), at(COL_K // gk)),
                pl.BlockSpec((t, GLA_WIDTH), at(COL_V // GLA_WIDTH)),
                pl.BlockSpec((t, LANES), at(COL_KRA // LANES)),
                pl.BlockSpec((LANES, gk), lambda b, i: (0, direction)),
                pl.BlockSpec((1, gk), lambda b, i: (0, direction))]

    o_f = pl.pallas_call(
        _gla_fwd_kernel,
        in_specs=z_specs(fwd, 0),
        out_specs=pl.BlockSpec((t, GLA_WIDTH), fwd(0)),
        out_shape=jax.ShapeDtypeStruct((n, GLA_WIDTH), F32),
        **common,
    )(z, z, z, z, wg2, bg2)
    return pl.pallas_call(
        _gla_bwd_kernel,
        in_specs=z_specs(rev, 1) + [pl.BlockSpec((t, GLA_WIDTH), rev(0)),
                                    pl.BlockSpec((t, GLA_WIDTH), rev(COL_R // GLA_WIDTH)),
                                    pl.BlockSpec((1, GLA_DV), lambda b, i: (0, 0))],
        out_specs=pl.BlockSpec((t, GLA_WIDTH), rev(0)),
        out_shape=jax.ShapeDtypeStruct((n, GLA_WIDTH), BF16),
        **common,
    )(z, z, z, z, wg2, bg2, o_f, z, norm_g.reshape(1, GLA_DV))


def _outproj_kernel(alpha, a_ref, b_ref, x_ref, woa_ref, wob_ref, g_ref, bb_ref, wr_ref, br_ref,
                    p_ref, wp_ref, wpg_ref, h_ref, acc_ref, aff_ref):
    mix = jnp.dot(a_ref[...], woa_ref[...], preferred_element_type=F32)
    mix = mix + jnp.dot(b_ref[...], wob_ref[...], preferred_element_type=F32)
    h = _layer_norm(alpha * x_ref[...] + mix, g_ref[...], bb_ref[...])
    h_ref[...] = h
    hb = h.astype(BF16)
    ple = jnp.dot(p_ref[...].astype(BF16), wp_ref[...], preferred_element_type=F32)
    ple_gate = jnp.dot(hb, wpg_ref[...], preferred_element_type=F32)
    acc_ref[...] = alpha * h + ple * jax.nn.sigmoid(ple_gate)
    h_lo = (h - hb.astype(F32)).astype(BF16)
    l1 = jnp.dot(hb, wr_ref[...], preferred_element_type=F32)
    l2 = jnp.dot(h_lo, wr_ref[:, :LANES], preferred_element_type=F32)
    logits = (l1[:, :LANES] + l1[:, LANES:] + l2)[:, :N_EXPERTS] + br_ref[...]
    e = jnp.exp(logits - jnp.max(logits, axis=-1, keepdims=True))
    aff_ref[...] = e / jnp.sum(e, axis=-1, keepdims=True)


def outproj_call(mix_a, mix_b, x, wo_a, wo_b, g, b, wr, br, p, wp, wpg, alpha, tm, layer):
    n, d = x.shape
    row = lambda i: (i, 0)
    const = lambda i: (0, 0)
    return pl.pallas_call(
        functools.partial(_outproj_kernel, alpha),
        grid=(n // tm,),
        in_specs=[pl.BlockSpec((tm, GLA_WIDTH), row),
                  pl.BlockSpec((tm, MLA_WIDTH), row),
                  pl.BlockSpec((tm, d), row),
                  _resident((GLA_WIDTH, d), const),
                  _resident((MLA_WIDTH, d), const),
                  pl.BlockSpec((1, d), const),
                  pl.BlockSpec((1, d), const),
                  _resident((d, 2 * LANES), const),
                  pl.BlockSpec((1, N_EXPERTS), const),
                  pl.BlockSpec((None, tm, PLE_DIM), lambda i: (layer, i, 0)),
                  _resident((PLE_DIM, d), const),
                  _resident((d, d), const)],
        out_specs=[pl.BlockSpec((tm, d), row), pl.BlockSpec((tm, d), row),
                   pl.BlockSpec((tm, N_EXPERTS), row)],
        out_shape=[jax.ShapeDtypeStruct((n, d), F32), jax.ShapeDtypeStruct((n, d), F32),
                   jax.ShapeDtypeStruct((n, N_EXPERTS), F32)],
        compiler_params=_params(("parallel",)),
    )(mix_a, mix_b, x, wo_a, wo_b, g.reshape(1, d), b.reshape(1, d), wr, br.reshape(1, N_EXPERTS), p, wp, wpg)


def _threshold_kernel(cap, a_ref, thr_ref, need_ref):
    bits = pltpu.bitcast(a_ref[...], I32)

    def count(pred):
        part = jnp.sum(pred.astype(I32), axis=1, keepdims=True)
        return jnp.sum(part, axis=2, keepdims=True)

    def body(i, thr):
        cand = thr | jnp.left_shift(jnp.int32(1), 30 - i)
        return jnp.where(count(bits >= cand) >= cap, cand, thr)

    thr = lax.fori_loop(0, 31, body, jnp.zeros((N_EXPERTS, 1, 1), I32))
    thr_ref[...] = jnp.broadcast_to(thr, thr_ref.shape)
    need_ref[...] = jnp.broadcast_to(cap - count(bits > thr), need_ref.shape)


def threshold_call(aff_t, cap):
    e, n = aff_t.shape
    full = lambda: (0, 0, 0)
    return pl.pallas_call(
        functools.partial(_threshold_kernel, cap),
        in_specs=[pl.BlockSpec((e, n // LANES, LANES), full)],
        out_specs=[pl.BlockSpec((e, 1, LANES), full), pl.BlockSpec((e, 1, LANES), full)],
        out_shape=[jax.ShapeDtypeStruct((e, 1, LANES), I32), jax.ShapeDtypeStruct((e, 1, LANES), I32)],
        compiler_params=_params(None),
    )(aff_t.reshape(e, n // LANES, LANES))


LIST_TID_HI, LIST_TID_LO, LIST_G0, LIST_G1, LIST_G2 = 0, 1, 2, 3, 4
TID_SHIFT = 8
TID_SPLIT = 1 << TID_SHIFT


LIST_GROUP = 8


def _list_kernel(thr_ref, need_ref, a_ref, u_ref, o_ref):
    e0 = pl.program_id(0) * LIST_GROUP
    _, nt, t = a_ref.shape
    o_ref[...] = jnp.zeros(o_ref.shape, F32)
    u = u_ref[...]
    sub8 = lax.broadcasted_iota(I32, (8, t), 0)
    subv = lax.broadcasted_iota(I32, (LANES, t), 0)
    slot = lax.broadcasted_iota(I32, (t, t), 0)
    lane = lax.broadcasted_iota(I32, (1, t), 1)

    def one_expert(g, j, gt_before, eq_before):
        thr = thr_ref[e0 + g]
        need = need_ref[e0 + g]
        a = a_ref[g, pl.ds(j, 1), :]
        bits = pltpu.bitcast(a, I32)
        gt = bits > thr
        eq = bits == thr
        lhs = jnp.where(sub8 == 0, jnp.where(gt, 1.0, 0.0), jnp.where(sub8 == 1, jnp.where(eq, 1.0, 0.0), 0.0))
        cs = jnp.dot(lhs.astype(BF16), u, preferred_element_type=F32).astype(I32)
        c_gt = cs[0:1, :]
        c_eq = cs[1:2, :]
        tie_rank = eq_before + c_eq
        sel = gt | (eq & (tie_rank <= need))
        off = gt_before + jnp.minimum(eq_before, need)
        dest = gt_before + c_gt + jnp.minimum(tie_rank, need) - 1 - off
        dest = jnp.where(sel, dest, -1)
        onehot = (slot == dest).astype(BF16)

        tid = j * t + lane
        g0 = a.astype(BF16).astype(F32)
        g1 = (a - g0).astype(BF16).astype(F32)
        g2 = (a - g0 - g1).astype(BF16).astype(F32)
        vals = jnp.where(subv == LIST_TID_HI, lax.shift_right_logical(tid, TID_SHIFT).astype(F32),
               jnp.where(subv == LIST_TID_LO, (tid & (TID_SPLIT - 1)).astype(F32),
               jnp.where(subv == LIST_G0, g0,
               jnp.where(subv == LIST_G1, g1,
               jnp.where(subv == LIST_G2, g2, 0.0)))))
        rows = lax.dot_general(onehot, vals.astype(BF16), (((1,), (1,)), ((), ())), preferred_element_type=F32)
        win = pl.ds(off, t)
        o_ref[g, win, :] = o_ref[g, win, :] + rows
        return gt_before + jnp.sum(gt.astype(I32)), eq_before + jnp.sum(eq.astype(I32))

    def tile(j, carry):
        out = []
        for g in range(LIST_GROUP):
            out.extend(one_expert(g, j, carry[2 * g], carry[2 * g + 1]))
        return tuple(out)

    lax.fori_loop(0, nt, tile, (jnp.int32(0),) * (2 * LIST_GROUP))


def list_call(thr, need, aff_t, cap, t):
    e, n = aff_t.shape
    cpad = cap + t
    u = (jnp.arange(t)[:, None] <= jnp.arange(t)[None, :]).astype(BF16)
    return pl.pallas_call(
        _list_kernel,
        grid_spec=pltpu.PrefetchScalarGridSpec(
            num_scalar_prefetch=2,
            grid=(e // LIST_GROUP,),
            in_specs=[pl.BlockSpec((LIST_GROUP, n // t, t), lambda ei, *_: (ei, 0, 0)),
                      pl.BlockSpec((t, t), lambda ei, *_: (0, 0))],
            out_specs=pl.BlockSpec((LIST_GROUP, cpad, LANES), lambda ei, *_: (ei, 0, 0))),
        out_shape=jax.ShapeDtypeStruct((e, cpad, LANES), F32),
        compiler_params=_params(("arbitrary",)),
    )(thr, need, aff_t.reshape(e, n // t, t), u)


def _ple_kernel(alpha, p_ref, hb_ref, h_ref, wp_ref, wg_ref, o_ref):
    ple = jnp.dot(p_ref[...].astype(BF16), wp_ref[...], preferred_element_type=F32)
    gate = jnp.dot(hb_ref[...], wg_ref[...], preferred_element_type=F32)
    o_ref[...] = alpha * h_ref[...] + ple * jax.nn.sigmoid(gate)


def ple_call(p, hb, h, wp, wg, alpha, tm, layer):
    n, d = h.shape
    row = lambda i: (i, 0)
    const = lambda i: (0, 0)
    return pl.pallas_call(
        functools.partial(_ple_kernel, alpha),
        grid=(n // tm,),
        in_specs=[pl.BlockSpec((None, tm, PLE_DIM), lambda i: (layer, i, 0)),
                  pl.BlockSpec((tm, d), row), pl.BlockSpec((tm, d), row),
                  _resident((PLE_DIM, d), const), _resident((d, d), const)],
        out_specs=pl.BlockSpec((tm, d), row),
        out_shape=jax.ShapeDtypeStruct((n, d), F32),
        compiler_params=_params(("parallel",)),
    )(p, hb, h, wp, wg)


def _moe_kernel(idx_ref, lst_ref, wg_ref, wu_ref, wd_ref, h_hbm, acc_in, acc_hbm,
                xbuf, xb_sc, abuf, xsem, asem, ssem):
    del acc_in
    tm = abuf.shape[0]
    nk = pl.num_programs(1)
    step = pl.program_id(0) * nk + pl.program_id(1)
    last = pl.num_programs(0) * nk - 1
    slot = step % 2

    def gather_x(s, sl):
        for r in range(tm):
            tok = idx_ref[s * tm + r]
            pltpu.make_async_copy(h_hbm.at[pl.ds(tok, 1), :], xbuf.at[sl, pl.ds(r, 1), :], xsem.at[sl]).start()

    def gather_acc(s):
        for r in range(tm):
            tok = idx_ref[s * tm + r]
            pltpu.make_async_copy(acc_hbm.at[pl.ds(tok, 1), :], abuf.at[pl.ds(r, 1), :], asem).start()

    def scatter_acc(s):
        for r in range(tm):
            tok = idx_ref[s * tm + r]
            pltpu.make_async_copy(abuf.at[pl.ds(r, 1), :], acc_hbm.at[pl.ds(tok, 1), :], ssem).start()

    def rows_done(buf, sem):
        pltpu.make_async_copy(h_hbm.at[pl.ds(0, tm), :], buf, sem).wait()

    @pl.when(step == 0)
    def _():
        gather_x(step, slot)
        gather_acc(step)
        rows_done(abuf, asem)

    rows_done(xbuf.at[slot], xsem.at[slot])
    xb_sc[...] = xbuf[slot].astype(BF16)
    gather_x(jnp.minimum(step + 1, last), 1 - slot)
    scatter_acc(jnp.maximum(step - 1, 0))
    xb = xb_sc[...]
    g = jnp.dot(xb, wg_ref[...], preferred_element_type=F32)
    u = jnp.dot(xb, wu_ref[...], preferred_element_type=F32)
    hid = (g * jax.nn.sigmoid(g) * u).astype(BF16)

    rows_done(abuf, ssem)
    gather_acc(step)
    y = jnp.dot(hid, wd_ref[...], preferred_element_type=F32)
    lst = lst_ref[...]
    gate = lst[:, LIST_G0:LIST_G0 + 1] + lst[:, LIST_G1:LIST_G1 + 1] + lst[:, LIST_G2:LIST_G2 + 1]
    y = y * gate
    rows_done(abuf, asem)
    abuf[...] = abuf[...] + y

    @pl.when(step == last)
    def _():
        scatter_acc(step)
        rows_done(abuf, ssem)
        rows_done(xbuf.at[1 - slot], xsem.at[1 - slot])


def moe_call(idx, lists, wg, wu, wd, h, acc, cap, tm, layer):
    n, d = h.shape
    e = wg.shape[1]
    ff = wg.shape[3]
    expert = lambda ei, k, *_: (layer, ei, 0, 0)
    return pl.pallas_call(
        _moe_kernel,
        grid_spec=pltpu.PrefetchScalarGridSpec(
            num_scalar_prefetch=1,
            grid=(e, cap // tm),
            in_specs=[pl.BlockSpec((None, tm, LANES), lambda ei, k, *_: (ei, k, 0)),
                      _resident((None, None, d, ff), expert),
                      _resident((None, None, d, ff), expert),
                      _resident((None, None, ff, d), expert),
                      pl.BlockSpec(memory_space=pl.ANY),
                      pl.BlockSpec(memory_space=pl.ANY)],
            out_specs=pl.BlockSpec(memory_space=pl.ANY),
            scratch_shapes=[pltpu.VMEM((2, tm, d), F32), pltpu.VMEM((tm, d), BF16), pltpu.VMEM((tm, d), F32),
                            pltpu.SemaphoreType.DMA((2,)), pltpu.SemaphoreType.DMA(()),
                            pltpu.SemaphoreType.DMA(())]),
        out_shape=jax.ShapeDtypeStruct((n, d), F32),
        input_output_aliases={6: 0},
        compiler_params=_params(("arbitrary", "arbitrary"), has_side_effects=True, disable_bounds_checks=True),
    )(idx, lists, wg, wu, wd, h, acc)


def _rotate_half_cols(w):
    half = MLA_ROPE // 2
    return jnp.concatenate([-w[..., half:], w[..., :half]], axis=-1)


def _prep_weights(w_in, gla_wg2_f, gla_bg_f, gla_wg2_b, gla_bg_b, mla_w_uq, w_o, w_gate_e, w_up_e,
                  w_down_e, w_ple, w_ple_gate):
    depth = w_in.shape[0]
    gk = GLA_HEADS * GLA_DK
    o = 0
    parts = {}
    for name, width in (("q", gk), ("k", gk), ("v", GLA_WIDTH), ("r", GLA_WIDTH), ("gf", GLA_GATE_RANK),
                        ("gb", GLA_GATE_RANK), ("cq", MLA_Q_RANK), ("ckv", MLA_KV_RANK), ("kr", MLA_ROPE)):
        parts[name] = w_in[:, :, o:o + width]
        o += width
    zeros = lambda w: jnp.zeros((depth, D_MODEL, w), w_in.dtype)
    kra = jnp.concatenate([parts["kr"], parts["gf"], parts["gb"],
                           zeros(LANES - MLA_ROPE - 2 * GLA_GATE_RANK)], axis=-1)
    krb = jnp.concatenate([_rotate_half_cols(parts["kr"]), zeros(LANES - MLA_ROPE)], axis=-1)
    w_in2 = jnp.concatenate([parts["q"], parts["k"], parts["v"], parts["r"], parts["cq"], kra, krb,
                             parts["ckv"]], axis=-1).astype(BF16)

    uq = mla_w_uq.reshape(depth, MLA_Q_RANK, MLA_HEADS, MLA_NOPE + MLA_ROPE)
    rope = uq[..., MLA_NOPE:]
    zpad = jnp.zeros(rope.shape[:-1] + (LANES - MLA_ROPE,), rope.dtype)
    wuq2 = jnp.concatenate([uq[..., :MLA_NOPE], rope, zpad, _rotate_half_cols(rope), zpad], axis=-1)
    wuq2 = wuq2.reshape(depth, MLA_Q_RANK, MLA_HEADS * MLA_Q_COLS).astype(BF16)

    wg2 = jnp.zeros((depth, LANES, 2 * gk), F32)
    wg2 = wg2.at[:, GATE_ROW_F:GATE_ROW_F + GLA_GATE_RANK, :gk].set(gla_wg2_f)
    wg2 = wg2.at[:, GATE_ROW_B:GATE_ROW_B + GLA_GATE_RANK, gk:].set(gla_wg2_b)
    bg2 = jnp.concatenate([gla_bg_f, gla_bg_b], axis=-1).reshape(depth, 1, 2 * gk)
    per_layer = lambda w: [w[i].astype(BF16) for i in range(depth)]
    return dict(w_in=w_in2, wuq=wuq2, wg2=wg2, bg2=bg2,
                wo_a=per_layer(w_o[:, :GLA_WIDTH]), wo_b=per_layer(w_o[:, GLA_WIDTH:]),
                wge=w_gate_e.astype(BF16), wue=w_up_e.astype(BF16), wde=w_down_e.astype(BF16),
                wple=per_layer(w_ple), wpg=per_layer(w_ple_gate))


def _rope_tables(seq):
    pos = jnp.arange(seq, dtype=F32)
    inv = ROPE_THETA ** (-jnp.arange(0, MLA_ROPE, 2, dtype=F32) / MLA_ROPE)
    ang = pos[:, None] * inv[None, :]
    zpad = jnp.zeros((seq, LANES - MLA_ROPE), F32)
    cos_t = jnp.concatenate([jnp.cos(ang), jnp.cos(ang), zpad], axis=-1)
    sin_t = jnp.concatenate([jnp.sin(ang), jnp.sin(ang), zpad], axis=-1)
    return cos_t, sin_t


def _tiles(batch, seq):
    n = batch * seq
    return dict(row=min(512, seq), mix=min(256, seq), gla=min(256, seq), tq=min(1024, seq),
                tk=min(8192, seq), lst=min(256, n), moe=min(256, CAPACITY_FACTOR * n // N_EXPERTS))


def _trunk(x, p, ln_in_g, ln_in_b, pw, gla_norm_g, mla_qnorm_g, mla_kvnorm_g, mla_w_ukv, ln1_g, ln1_b,
           w_router, b_router, ln2_g, ln2_b):
    batch, seq, d = x.shape
    depth = p.shape[0]
    n = batch * seq
    cap = CAPACITY_FACTOR * n // N_EXPERTS
    alpha = (2 * depth) ** 0.25
    tl = _tiles(batch, seq)
    cos_t, sin_t = _rope_tables(seq)
    wukv = mla_w_ukv.astype(BF16)
    wr_hi = w_router.astype(BF16)
    wr_lo = (w_router - wr_hi.astype(F32)).astype(BF16)
    wr_pad = jnp.zeros((depth, d, LANES - N_EXPERTS), BF16)
    wr2 = jnp.concatenate([wr_hi, wr_pad, wr_lo, wr_pad], axis=-1)
    xf, xb = layer_norm_call(x.reshape(n, d), ln_in_g, ln_in_b, tl["row"])
    for i in range(depth):
        z = inproj_call(xb, pw["w_in"][i], tl["row"])
        mix_a = gla_call(z, pw["wg2"][i], pw["bg2"][i], gla_norm_g[i], batch, seq, tl["gla"])
        q, k, v = mla_proj_call(z, cos_t, sin_t, mla_qnorm_g[i], mla_kvnorm_g[i], pw["wuq"][i], wukv[i],
                                seq, tl["row"])
        mix_b = flash_call(q, k, v, batch, seq, tl["tq"], tl["tk"])
        h, acc, aff = outproj_call(mix_a, mix_b, xf, pw["wo_a"][i], pw["wo_b"][i], ln1_g[i], ln1_b[i],
                                   wr2[i], b_router[i], p.reshape(depth, n, PLE_DIM), pw["wple"][i],
                                   pw["wpg"][i], alpha, tl["mix"], i)
        aff_t = aff.T
        thr, need = threshold_call(aff_t, cap)
        lists = list_call(thr[:, 0, 0], need[:, 0, 0], aff_t, cap, tl["lst"])
        idx = (lists[:, :cap, LIST_TID_HI] * TID_SPLIT + lists[:, :cap, LIST_TID_LO]).astype(I32).reshape(-1)
        acc = moe_call(idx, lists, pw["wge"], pw["wue"], pw["wde"], h, acc, cap, tl["moe"], i)
        xf, xb = layer_norm_call(acc, ln2_g[i], ln2_b[i], tl["row"])
    return xf.reshape(batch, seq, d)


def kernel(x_prompt, x_sample, p_prompt, p_sample, ln_in_g, ln_in_b, w_in, gla_wg2_f, gla_bg_f, gla_wg2_b, gla_bg_b, gla_norm_g, mla_qnorm_g, mla_kvnorm_g, mla_w_uq, mla_w_ukv, w_o, ln1_g, ln1_b, w_router, b_router, w_gate_e, w_up_e, w_down_e, w_ple, w_ple_gate, ln2_g, ln2_b):
    pw = _prep_weights(w_in, gla_wg2_f, gla_bg_f, gla_wg2_b, gla_bg_b, mla_w_uq, w_o, w_gate_e, w_up_e,
                       w_down_e, w_ple, w_ple_gate)
    rest = (ln_in_g, ln_in_b, pw, gla_norm_g, mla_qnorm_g, mla_kvnorm_g, mla_w_ukv, ln1_g, ln1_b,
            w_router, b_router, ln2_g, ln2_b)
    return (_trunk(x_prompt, p_prompt, *rest), _trunk(x_sample, p_sample, *rest))
```
